```python
import jax
import jax.numpy as jnp
from jax import lax
import numpy as np

D_MODEL = 2048
BATCH = 16
SEQ = 2048
DEPTH = 4

GRID_W = 64
CTX_LEN = 256
N_MIXERS = 3
N_MOD = 6
EPS = 1e-6

HEAD_DIM = 128
N_Q_HEADS = D_MODEL // HEAD_DIM
N_KV_HEADS = max(N_Q_HEADS // 4, 1)
Q_PER_KV = N_Q_HEADS // N_KV_HEADS
WINDOW = 128
BLOCK = 128
ROPE_BASE = 10000.0

GMLP_CHUNK = 128
GMLP_WIDTH = 2 * D_MODEL
GMLP_GROUPS = 16

POOL_SIZES = (2, 4, 8, 16)
POOL_GROUPS = len(POOL_SIZES)

D_FF = 4 * D_MODEL

N_ATTN = (DEPTH + 2) // 3
N_GMLP = (DEPTH + 1) // 3
N_POOL = DEPTH // 3

kernel_name = 'hybrid_interleaved_diffusion_backbone'


def rmsnorm(x, g):
    xf = x.astype(jnp.float32)
    y = xf * lax.rsqrt(jnp.mean(xf * xf, axis=-1, keepdims=True) + EPS)
    return (y * g.astype(jnp.float32)).astype(x.dtype)


def layernorm(x, g):
    xf = x.astype(jnp.float32)
    xc = xf - jnp.mean(xf, axis=-1, keepdims=True)
    y = xc * lax.rsqrt(jnp.mean(xc * xc, axis=-1, keepdims=True) + EPS)
    return (y * g.astype(jnp.float32)).astype(x.dtype)


def axial_rope_tables(n_tokens):
    n_rows = n_tokens // GRID_W
    row = jnp.broadcast_to(jnp.arange(n_rows)[:, None], (n_rows, GRID_W)).reshape(-1)
    col = jnp.broadcast_to(jnp.arange(GRID_W)[None, :], (n_rows, GRID_W)).reshape(-1)
    axis_dim = HEAD_DIM // 2
    inv_freq = ROPE_BASE ** (-jnp.arange(0, axis_dim, 2, dtype=jnp.float32) / axis_dim)
    ang = jnp.stack([row.astype(jnp.float32)[:, None] * inv_freq,
                     col.astype(jnp.float32)[:, None] * inv_freq], axis=1)
    return jnp.cos(ang), jnp.sin(ang)


def apply_axial_rope(x, cos, sin):
    q = HEAD_DIM // 4
    xr = x.reshape(x.shape[:-1] + (2, 2, q))
    x1, x2 = xr[..., 0, :], xr[..., 1, :]
    bshape = (cos.shape[0],) + (1,) * (x.ndim - 3) + (2, q)
    cs = cos.reshape(bshape).astype(x.dtype)
    sn = sin.reshape(bshape).astype(x.dtype)
    out = jnp.stack([x1 * cs - x2 * sn, x2 * cs + x1 * sn], axis=-2)
    return out.reshape(x.shape)


def windowed_gqa_sink(h, hc, wq, wk, wv, wo, sink, with_ctx_out):
    B, S, _ = h.shape
    L = hc.shape[1]
    nb = S // BLOCK
    scale = HEAD_DIM ** -0.5
    cos, sin = axial_rope_tables(S)
    q = apply_axial_rope((h @ wq).reshape(B, S, N_KV_HEADS, Q_PER_KV, HEAD_DIM), cos, sin)
    k = apply_axial_rope((h @ wk).reshape(B, S, N_KV_HEADS, HEAD_DIM), cos, sin)
    v = (h @ wv).reshape(B, S, N_KV_HEADS, HEAD_DIM)
    kc = (hc @ wk).reshape(B, L, N_KV_HEADS, HEAD_DIM)
    vc = (hc @ wv).reshape(B, L, N_KV_HEADS, HEAD_DIM)
    sink_f = sink.astype(jnp.float32).reshape(N_KV_HEADS, Q_PER_KV, 1, 1)

    def band(t):
        tp = jnp.pad(t, ((0, 0), (BLOCK, BLOCK), (0, 0), (0, 0)))
        tp = tp.reshape(B, nb + 2, BLOCK, N_KV_HEADS, HEAD_DIM)
        tb = jnp.concatenate([tp[:, :-2], tp[:, 1:-1], tp[:, 2:]], axis=2)
        return jnp.moveaxis(tb, 1, 0)

    qb = jnp.moveaxis(q.reshape(B, nb, BLOCK, N_KV_HEADS, Q_PER_KV, HEAD_DIM), 1, 0)
    kb, vb = band(k), band(v)
    qi = jnp.arange(BLOCK)[:, None]
    kj = jnp.arange(3 * BLOCK)[None, :]
    key_pos = (jnp.arange(nb)[:, None, None] - 1) * BLOCK + kj[None]
    valid = (jnp.abs(kj - BLOCK - qi)[None] <= WINDOW) & (key_pos >= 0) & (key_pos < S)

    def attend_block(args):
        q_blk, k_blk, v_blk, mask = args
        s_win = jnp.einsum('bqhgd,bkhd->bhgqk', q_blk, k_blk).astype(jnp.float32) * scale
        s_win = jnp.where(mask, s_win, -jnp.inf)
        s_ctx = jnp.einsum('bqhgd,bkhd->bhgqk', q_blk, kc).astype(jnp.float32) * scale
        s_sink = jnp.broadcast_to(sink_f, s_win.shape[:-1] + (1,))
        p = jax.nn.softmax(jnp.concatenate([s_win, s_ctx, s_sink], axis=-1), axis=-1).astype(v_blk.dtype)
        o = jnp.einsum('bhgqk,bkhd->bqhgd', p[..., :3 * BLOCK], v_blk)
        return o + jnp.einsum('bhgqk,bkhd->bqhgd', p[..., 3 * BLOCK:3 * BLOCK + L], vc)

    o = lax.map(attend_block, (qb, kb, vb, valid))
    y = jnp.moveaxis(o, 0, 1).reshape(B, S, N_Q_HEADS * HEAD_DIM) @ wo
    if not with_ctx_out:
        return y, None
    qc = (hc @ wq).reshape(B, L, N_KV_HEADS, Q_PER_KV, HEAD_DIM)
    sc = jnp.einsum('bqhgd,bkhd->bhgqk', qc, kc).astype(jnp.float32) * scale
    sc = jnp.concatenate([sc, jnp.broadcast_to(sink_f, sc.shape[:-1] + (1,))], axis=-1)
    pc = jax.nn.softmax(sc, axis=-1).astype(vc.dtype)
    oc = jnp.einsum('bhgqk,bkhd->bqhgd', pc[..., :L], vc)
    return y, oc.reshape(B, L, N_Q_HEADS * HEAD_DIM) @ wo


def chunk_gmlp(h, w_in, g_v, w_s, b_s, w_out):
    B, S, _ = h.shape
    nc = S // GMLP_CHUNK
    cg = GMLP_WIDTH // GMLP_GROUPS
    uv = jax.nn.gelu(h @ w_in)
    u, v = uv[..., :GMLP_WIDTH], uv[..., GMLP_WIDTH:]
    v = layernorm(v, g_v).reshape(B, nc, GMLP_CHUNK, GMLP_GROUPS, cg)
    sv = jnp.einsum('gpq,bnqgc->bnpgc', w_s, v) + jnp.transpose(b_s)[None, None, :, :, None]
    return (u * sv.reshape(B, S, GMLP_WIDTH)) @ w_out


def multiscale_pool(h, w_pool, scale):
    B, S, D = h.shape
    cg = D // POOL_GROUPS
    csum = jnp.pad(jnp.cumsum(h.astype(jnp.float32), axis=1), ((0, 0), (1, 0), (0, 0)))
    t = jnp.arange(S)
    pooled = []
    for gi, w in enumerate(POOL_SIZES):
        lo = jnp.clip(t - w // 2, 0, S)
        hi = jnp.clip(t + w // 2, 0, S)
        cs_g = csum[:, :, gi * cg:(gi + 1) * cg]
        cnt = (hi - lo).astype(jnp.float32)[None, :, None]
        pooled.append((cs_g[:, hi] - cs_g[:, lo]) / cnt)
    d = jnp.concatenate(pooled, axis=-1).astype(h.dtype) - h
    y = jnp.einsum('bsgc,gcd->bsgd', d.reshape(B, S, POOL_GROUPS, cg), w_pool)
    return y.reshape(B, S, D) * scale


def sqrelu_mlp(h, w_up, w_down):
    return jnp.square(jax.nn.relu(h @ w_up)) @ w_down


def setup_inputs(seed: int = 0) -> dict:
    key = jax.random.key(seed)
    ks = jax.random.split(key, 22)
    D = D_MODEL
    HQD = N_Q_HEADS * HEAD_DIM
    HKD = N_KV_HEADS * HEAD_DIM
    cg = D // POOL_GROUPS

    def normal(k, shape, std=1.0):
        return jax.random.normal(k, shape, jnp.float32) * std

    return {
        'x': normal(ks[0], (BATCH, SEQ, D)),
        'c': normal(ks[1], (BATCH, D)),
        'ctx': normal(ks[2], (BATCH, CTX_LEN, D)),
        'c_ctx': normal(ks[3], (D,)),
        'w_mod': normal(ks[4], (DEPTH, D, N_MOD * D), 0.5 * D ** -0.5),
        'b_mod': normal(ks[5], (DEPTH, N_MOD * D), 0.02),
        'g_norm': 1.0 + normal(ks[6], (DEPTH, 2, D), 0.05),
        'w_up': normal(ks[7], (DEPTH, D, D_FF), D ** -0.5),
        'w_down': normal(ks[8], (DEPTH, D_FF, D), D_FF ** -0.5),
        'attn_wq': normal(ks[9], (N_ATTN, D, HQD), D ** -0.5),
        'attn_wk': normal(ks[10], (N_ATTN, D, HKD), D ** -0.5),
        'attn_wv': normal(ks[11], (N_ATTN, D, HKD), D ** -0.5),
        'attn_wo': normal(ks[12], (N_ATTN, HQD, D), HQD ** -0.5),
        'attn_sink': normal(ks[13], (N_ATTN, N_Q_HEADS), 0.5),
        'gmlp_w_in': normal(ks[14], (N_GMLP, D, 2 * GMLP_WIDTH), D ** -0.5),
        'gmlp_g_v': 1.0 + normal(ks[15], (N_GMLP, GMLP_WIDTH), 0.05),
        'gmlp_w_s': normal(ks[16], (N_GMLP, GMLP_GROUPS, GMLP_CHUNK, GMLP_CHUNK), GMLP_CHUNK ** -0.5),
        'gmlp_b_s': 1.0 + normal(ks[17], (N_GMLP, GMLP_GROUPS, GMLP_CHUNK), 0.1),
        'gmlp_w_out': normal(ks[18], (N_GMLP, GMLP_WIDTH, D), GMLP_WIDTH ** -0.5),
        'pool_w': normal(ks[19], (N_POOL, POOL_GROUPS, cg, cg), cg ** -0.5),
        'pool_scale': 1.0 + normal(ks[20], (N_POOL, D), 0.1),
        'g_final': 1.0 + normal(ks[21], (D,), 0.05),
    }


def reference(x, c, ctx, c_ctx, w_mod, b_mod, g_norm, w_up, w_down,
              attn_wq, attn_wk, attn_wv, attn_wo, attn_sink,
              gmlp_w_in, gmlp_g_v, gmlp_w_s, gmlp_b_s, gmlp_w_out,
              pool_w, pool_scale, g_final):
    B = x.shape[0]
    D = x.shape[-1]
    s_lat = jax.nn.silu(c)
    s_ctx = jax.nn.silu(c_ctx)
    for i in range(DEPTH):
        kind, j = i % N_MIXERS, i // N_MIXERS
        last = i == DEPTH - 1
        need_ctx = (not last) or kind == 0
        m = (s_lat @ w_mod[i] + b_mod[i]).reshape(B, N_MOD, 1, D)
        sh_a, sc_a, gt_a, sh_m, sc_m, gt_m = (m[:, k] for k in range(N_MOD))
        h = rmsnorm(x, g_norm[i, 0]) * (1 + sc_a) + sh_a
        hc = None
        if need_ctx:
            mc = (s_ctx @ w_mod[i] + b_mod[i]).reshape(N_MOD, 1, 1, D)
            csh_a, csc_a, cgt_a, csh_m, csc_m, cgt_m = (mc[k] for k in range(N_MOD))
            hc = rmsnorm(ctx, g_norm[i, 0]) * (1 + csc_a) + csh_a
        if kind == 0:
            y, yc = windowed_gqa_sink(h, hc, attn_wq[j], attn_wk[j], attn_wv[j], attn_wo[j],
                                      attn_sink[j], not last)
        elif kind == 1:
            y = chunk_gmlp(h, gmlp_w_in[j], gmlp_g_v[j], gmlp_w_s[j], gmlp_b_s[j], gmlp_w_out[j])
            yc = None if last else chunk_gmlp(hc, gmlp_w_in[j], gmlp_g_v[j], gmlp_w_s[j],
                                              gmlp_b_s[j], gmlp_w_out[j])
        else:
            y = multiscale_pool(h, pool_w[j], pool_scale[j])
            yc = None if last else multiscale_pool(hc, pool_w[j], pool_scale[j])
        x = x + gt_a * y
        x = x + gt_m * sqrelu_mlp(rmsnorm(x, g_norm[i, 1]) * (1 + sc_m) + sh_m, w_up[i], w_down[i])
        if not last:
            ctx = ctx + cgt_a * yc
            ctx = ctx + cgt_m * sqrelu_mlp(rmsnorm(ctx, g_norm[i, 1]) * (1 + csc_m) + csh_m,
                                           w_up[i], w_down[i])
    return rmsnorm(x, g_final)
```

```python
import functools

import jax
import jax.numpy as jnp
from jax import lax
from jax.experimental import pallas as pl
from jax.experimental.pallas import tpu as pltpu

EPS = 1e-6
N_MOD = 6
N_MIXERS = 3
GRID_W = 64
ATT_BLOCK = 128
ROPE_BASE = 10000.0
POOL_SIZES = (2, 4, 8, 16)
POOL_HALO = 8
LANE = 128
SUBLANE = 8
VMEM_LIMIT_BYTES = 56 * 1024 * 1024

BF16 = jnp.bfloat16
F32 = jnp.float32


def _params(*sem):
    return pltpu.CompilerParams(dimension_semantics=sem, vmem_limit_bytes=VMEM_LIMIT_BYTES)


def _normmod(x, g, shift, scale):
    ms = jnp.mean(x * x, axis=-1, keepdims=True)
    return (x * lax.rsqrt(ms + EPS)) * g * (1.0 + scale) + shift


ROW_CHUNK = 128
MLP_OUT_CHUNK = 512


def _for_row_chunks(n_rows, body):
    def step(i, carry):
        body(pl.ds(pl.multiple_of(i * ROW_CHUNK, ROW_CHUNK), ROW_CHUNK))
        return carry
    lax.fori_loop(0, n_rows // ROW_CHUNK, step, 0)


def _normmod_into(dst_ref, x_ref, g_ref, mod_ref, shift_row):
    def body(rows):
        h = _normmod(x_ref[rows, :], g_ref[...], mod_ref[shift_row:shift_row + 1, :],
                     mod_ref[shift_row + 1:shift_row + 2, :])
        dst_ref[rows, :] = h.astype(dst_ref.dtype)
    _for_row_chunks(x_ref.shape[0], body)


def _mod_kernel(c_ref, w_ref, b_ref, o_ref):
    c = c_ref[...]
    s = (c * jax.nn.sigmoid(c)).astype(BF16)
    o_ref[...] = jnp.dot(s, w_ref[...].astype(BF16), preferred_element_type=F32) + b_ref[...]


def _modulation(cond, w_mod, b_mod):
    depth, d, n = w_mod.shape
    r = cond.shape[0]
    tn = 1024
    return pl.pallas_call(
        _mod_kernel,
        grid=(depth, n // tn),
        in_specs=[
            pl.BlockSpec((r, d), lambda i, j: (0, 0)),
            pl.BlockSpec((None, d, tn), lambda i, j: (i, 0, j)),
            pl.BlockSpec((None, 1, tn), lambda i, j: (i, 0, j)),
        ],
        out_specs=pl.BlockSpec((None, r, tn), lambda i, j: (i, 0, j)),
        out_shape=jax.ShapeDtypeStruct((depth, r, n), F32),
        compiler_params=_params("parallel", "parallel"),
        name="modulation",
    )(cond, w_mod, b_mod.reshape(depth, 1, n))


def _mlp_kernel(x_ref, mod_ref, g_ref, wu_ref, wd_ref, gf_ref, o_ref, h_ref, *, final):
    f = pl.program_id(1)

    @pl.when(f == 0)
    def _():
        _normmod_into(h_ref, x_ref, g_ref, mod_ref, 3)

        def zero(rows):
            o_ref[rows, :] = jnp.zeros((ROW_CHUNK, o_ref.shape[1]), F32)
        _for_row_chunks(o_ref.shape[0], zero)

    u = jnp.dot(h_ref[...], wu_ref[...], preferred_element_type=F32)
    u = jnp.square(jnp.maximum(u, 0.0)).astype(BF16)
    d = o_ref.shape[1]
    for n0 in range(0, d, MLP_OUT_CHUNK):
        cols = slice(n0, n0 + MLP_OUT_CHUNK)
        o_ref[:, cols] += jnp.dot(u, wd_ref[:, cols], preferred_element_type=F32)

    @pl.when(f == pl.num_programs(1) - 1)
    def _():
        def finish(rows):
            y = x_ref[rows, :] + mod_ref[5:6, :] * o_ref[rows, :]
            if final:
                ms = jnp.mean(y * y, axis=-1, keepdims=True)
                y = (y * lax.rsqrt(ms + EPS)) * gf_ref[...]
            o_ref[rows, :] = y
        _for_row_chunks(o_ref.shape[0], finish)


def _mlp(xs, mod, g, w_up, w_down, g_final, *, rows, tm, mod_index, final):
    d, ff = w_up.shape
    tf = 512
    out_rows = rows if final else xs.shape[0]
    return pl.pallas_call(
        functools.partial(_mlp_kernel, final=final),
        grid=(rows // tm, ff // tf),
        in_specs=[
            pl.BlockSpec((tm, d), lambda t, f: (t, 0)),
            pl.BlockSpec((None, N_MOD, d), lambda t, f: (mod_index(t), 0, 0)),
            pl.BlockSpec((1, d), lambda t, f: (0, 0)),
            pl.BlockSpec((d, tf), lambda t, f: (0, f)),
            pl.BlockSpec((tf, d), lambda t, f: (f, 0)),
            pl.BlockSpec((1, d), lambda t, f: (0, 0)),
        ],
        out_specs=pl.BlockSpec((tm, d), lambda t, f: (t, 0)),
        out_shape=jax.ShapeDtypeStruct((out_rows, d), F32),
        scratch_shapes=[pltpu.VMEM((tm, d), BF16)],
        input_output_aliases={} if final else {0: 0},
        compiler_params=_params("parallel", "arbitrary"),
        name="mlp_final" if final else "mlp",
    )(xs, mod, g, w_up, w_down, g_final)


def _qkv_kernel(x_ref, mod_ref, g_ref, w_ref, cq_ref, sq_ref, ck_ref, sk_ref,
                q_ref, k_ref, v_ref, h_ref, *, n_q, n_kv, hd):
    _normmod_into(h_ref, x_ref, g_ref, mod_ref, 0)
    h = h_ref[...]

    def rope(y, c_ref, s_ref):
        return y * c_ref[...] + pltpu.roll(y, hd // 2, axis=1) * s_ref[...]

    group = 4
    for h0 in range(0, n_q, group):
        acc = jnp.dot(h, w_ref[:, h0 * hd:(h0 + group) * hd], preferred_element_type=F32)
        for j in range(group):
            y = acc[:, j * hd:(j + 1) * hd]
            q_ref[:, (h0 + j) * hd:(h0 + j + 1) * hd] = rope(y, cq_ref, sq_ref).astype(BF16)
    acc = jnp.dot(h, w_ref[:, n_q * hd:(n_q + n_kv) * hd], preferred_element_type=F32)
    for j in range(n_kv):
        k_ref[:, j * hd:(j + 1) * hd] = rope(acc[:, j * hd:(j + 1) * hd], ck_ref, sk_ref).astype(BF16)
    acc = jnp.dot(h, w_ref[:, (n_q + n_kv) * hd:(n_q + 2 * n_kv) * hd], preferred_element_type=F32)
    v_ref[...] = acc.astype(BF16)


def _qkv(xs, mod, g, wqkv, tables, *, tm, n_q, n_kv, hd, mod_index, table_index):
    rows, d = xs.shape
    tab_spec = pl.BlockSpec((None, tm, hd), lambda t: (table_index(t), 0, 0))
    return pl.pallas_call(
        functools.partial(_qkv_kernel, n_q=n_q, n_kv=n_kv, hd=hd),
        grid=(rows // tm,),
        in_specs=[
            pl.BlockSpec((tm, d), lambda t: (t, 0)),
            pl.BlockSpec((None, N_MOD, d), lambda t: (mod_index(t), 0, 0)),
            pl.BlockSpec((1, d), lambda t: (0, 0)),
            pl.BlockSpec(wqkv.shape, lambda t: (0, 0)),
            tab_spec, tab_spec, tab_spec, tab_spec,
        ],
        out_specs=[
            pl.BlockSpec((tm, n_q * hd), lambda t: (t, 0)),
            pl.BlockSpec((tm, n_kv * hd), lambda t: (t, 0)),
            pl.BlockSpec((tm, n_kv * hd), lambda t: (t, 0)),
        ],
        out_shape=[
            jax.ShapeDtypeStruct((rows, n_q * hd), BF16),
            jax.ShapeDtypeStruct((rows, n_kv * hd), BF16),
            jax.ShapeDtypeStruct((rows, n_kv * hd), BF16),
        ],
        scratch_shapes=[pltpu.VMEM((tm, d), BF16)],
        compiler_params=_params("parallel"),
        name="qkv_rope",
    )(xs, mod, g, wqkv, *tables)


def _attn_kernel(sink_ref, q_ref, kp_ref, kc_ref, kn_ref, kx_ref, vp_ref, vc_ref, vn_ref, vx_ref,
                 bias_ref, o_ref, *, n_kv, grp, hd):
    bias = bias_ref[...]
    blk = q_ref.shape[0]
    for hk in range(n_kv):
        sl = slice(hk * hd, (hk + 1) * hd)
        k_all = jnp.concatenate([kp_ref[:, sl], kc_ref[:, sl], kn_ref[:, sl], kx_ref[:, sl]], axis=0)
        v_all = jnp.concatenate([vp_ref[:, sl], vc_ref[:, sl], vn_ref[:, sl], vx_ref[:, sl]], axis=0)
        heads = [hk * grp + j for j in range(grp)]
        qh = jnp.concatenate([q_ref[:, hq * hd:(hq + 1) * hd] for hq in heads], axis=0)
        s = lax.dot_general(qh, k_all, (((1,), (1,)), ((), ())), preferred_element_type=F32) + bias
        sink = jnp.concatenate([jnp.full((blk, 1), sink_ref[hq], F32) for hq in heads], axis=0)
        m = jnp.maximum(jnp.max(s, axis=-1, keepdims=True), sink)
        p = jnp.exp(s - m)
        denom = jnp.sum(p, axis=-1, keepdims=True) + jnp.exp(sink - m)
        o = jnp.dot(p.astype(BF16), v_all, preferred_element_type=F32) / denom
        for j, hq in enumerate(heads):
            o_ref[:, hq * hd:(hq + 1) * hd] = o[j * blk:(j + 1) * blk, :].astype(o_ref.dtype)


def _attention(q, k, v, sink, bias, *, batch, seq, ctx_len, n_q, n_kv, hd, ctx_out):
    rows = q.shape[0]
    blk = ATT_BLOCK
    nb = seq // blk
    ncb = ctx_len // blk
    lat_blocks = batch * nb
    nq_blocks = nb + (ncb if ctx_out else 0)

    def q_idx(b, n):
        return (jnp.where(n < nb, b * nb + n, lat_blocks + b * ncb + (n - nb)), 0)

    def kv_idx(off):
        return lambda b, n: (b * nb + jnp.clip(n + off, 0, nb - 1), 0)

    def ctx_idx(b, n):
        return (lat_blocks * blk // ctx_len + b, 0)

    def bias_idx(b, n):
        return (jnp.where(n >= nb, 3, jnp.where(n == 0, 0, jnp.where(n == nb - 1, 2, 1))), 0, 0)

    kv_w = n_kv * hd
    kv_specs = [pl.BlockSpec((blk, kv_w), kv_idx(-1)), pl.BlockSpec((blk, kv_w), kv_idx(0)),
                pl.BlockSpec((blk, kv_w), kv_idx(1)), pl.BlockSpec((ctx_len, kv_w), ctx_idx)]
    return pl.pallas_call(
        functools.partial(_attn_kernel, n_kv=n_kv, grp=n_q // n_kv, hd=hd),
        grid=(batch, nq_blocks),
        in_specs=[pl.BlockSpec(memory_space=pltpu.SMEM),
                  pl.BlockSpec((blk, n_q * hd), q_idx)] + kv_specs + kv_specs
                 + [pl.BlockSpec((None,) + bias.shape[1:], bias_idx)],
        out_specs=pl.BlockSpec((blk, n_q * hd), q_idx),
        out_shape=jax.ShapeDtypeStruct((rows, n_q * hd), BF16),
        compiler_params=_params("parallel", "parallel"),
        name="window_attention",
    )(sink, q, k, k, k, k, v, v, v, v, bias)


def _proj_res_kernel(a_ref, w_ref, x_ref, mod_ref, o_ref):
    y = jnp.dot(a_ref[...], w_ref[...], preferred_element_type=F32)
    o_ref[...] = x_ref[...] + mod_ref[2:3, :] * y


def _proj_residual(a, w, xs, mod, *, rows, tm, mod_index):
    kdim, d = w.shape
    tn = 1024
    return pl.pallas_call(
        _proj_res_kernel,
        grid=(rows // tm, d // tn),
        in_specs=[
            pl.BlockSpec((tm, kdim), lambda t, j: (t, 0)),
            pl.BlockSpec((kdim, tn), lambda t, j: (0, j)),
            pl.BlockSpec((tm, tn), lambda t, j: (t, j)),
            pl.BlockSpec((None, N_MOD, tn), lambda t, j: (mod_index(t), 0, j)),
        ],
        out_specs=pl.BlockSpec((tm, tn), lambda t, j: (t, j)),
        out_shape=jax.ShapeDtypeStruct(xs.shape, F32),
        input_output_aliases={2: 0},
        compiler_params=_params("parallel", "arbitrary"),
        name="proj_residual",
    )(a, w, xs, mod)


def _rope_tables(seq, tm, hd, scale):
    q = hd // 4
    pos = jnp.arange(seq)
    row = (pos // GRID_W).astype(F32)[:, None]
    col = (pos % GRID_W).astype(F32)[:, None]
    axis_dim = hd // 2
    inv_freq = ROPE_BASE ** (-jnp.arange(0, axis_dim, 2, dtype=F32) / axis_dim)
    a_row, a_col = row * inv_freq, col * inv_freq
    cos = jnp.concatenate([jnp.cos(a_row), jnp.cos(a_col)] * 2, axis=1)
    sin = jnp.concatenate([-jnp.sin(a_row), -jnp.sin(a_col), jnp.sin(a_row), jnp.sin(a_col)], axis=1)
    assert cos.shape == (seq, 4 * q)
    cos = jnp.concatenate([cos.reshape(seq // tm, tm, hd), jnp.ones((1, tm, hd), F32)], axis=0)
    sin = jnp.concatenate([sin.reshape(seq // tm, tm, hd), jnp.zeros((1, tm, hd), F32)], axis=0)
    return cos * scale, sin * scale, cos, sin


def _head_perm(n_heads, hd):
    q = hd // 4
    one = jnp.concatenate([jnp.arange(0, q), jnp.arange(2 * q, 3 * q), jnp.arange(q, 2 * q), jnp.arange(3 * q, 4 * q)])
    return (jnp.arange(n_heads)[:, None] * hd + one[None, :]).reshape(-1)


def _attn_bias(grp, ctx_len):
    blk = ATT_BLOCK
    qi = jnp.arange(grp * blk)[:, None] % blk
    kj = jnp.arange(3 * blk + ctx_len)[None, :]
    is_ctx = kj >= 3 * blk
    win = jnp.abs(kj - blk - qi) <= ATT_BLOCK
    variants = [win & (kj >= blk), win, win & (kj < 2 * blk), jnp.zeros_like(win)]
    return jnp.stack([jnp.where(is_ctx | m, 0.0, -jnp.inf).astype(F32) for m in variants])


def _norm_proj_gelu_kernel(x_ref, mod_ref, g_ref, w_ref, o_ref, h_ref):
    @pl.when(pl.program_id(1) == 0)
    def _():
        _normmod_into(h_ref, x_ref, g_ref, mod_ref, 0)

    h = h_ref[...]
    for n0 in range(0, o_ref.shape[1], MLP_OUT_CHUNK):
        cols = slice(n0, n0 + MLP_OUT_CHUNK)
        y = jnp.dot(h, w_ref[:, cols], preferred_element_type=F32)
        o_ref[:, cols] = jax.nn.gelu(y).astype(o_ref.dtype)


def _norm_proj_gelu(xs, mod, g, w, *, tm, mod_index):
    rows, d = xs.shape
    n = w.shape[1]
    tn = 1024
    return pl.pallas_call(
        _norm_proj_gelu_kernel,
        grid=(rows // tm, n // tn),
        in_specs=[
            pl.BlockSpec((tm, d), lambda t, j: (t, 0)),
            pl.BlockSpec((None, N_MOD, d), lambda t, j: (mod_index(t), 0, 0)),
            pl.BlockSpec((1, d), lambda t, j: (0, 0)),
            pl.BlockSpec((d, tn), lambda t, j: (0, j)),
        ],
        out_specs=pl.BlockSpec((tm, tn), lambda t, j: (t, j)),
        out_shape=jax.ShapeDtypeStruct((rows, n), BF16),
        scratch_shapes=[pltpu.VMEM((tm, d), BF16)],
        compiler_params=_params("parallel", "arbitrary"),
        name="gmlp_in",
    )(xs, mod, g, w)


def _gate_proj_kernel(u_ref, v_ref, gv_ref, ws_ref, bs_ref, wo_ref, x_ref, mod_ref, o_ref, z_ref,
                      *, groups, chunk):
    @pl.when(pl.program_id(1) == 0)
    def _():
        cg = v_ref.shape[1] // groups

        def gate_chunk(c, carry):
            rws = pl.ds(pl.multiple_of(c * chunk, chunk), chunk)
            v = v_ref[rws, :].astype(F32)
            vc = v - jnp.mean(v, axis=-1, keepdims=True)
            var = jnp.mean(vc * vc, axis=-1, keepdims=True)
            vn = ((vc * lax.rsqrt(var + EPS)) * gv_ref[...]).astype(BF16)
            for gi in range(groups):
                cols = slice(gi * cg, (gi + 1) * cg)
                sv = jnp.dot(ws_ref[gi], vn[:, cols], preferred_element_type=F32) + bs_ref[:, gi:gi + 1]
                z_ref[rws, cols] = (u_ref[rws, cols].astype(F32) * sv).astype(BF16)
            return carry
        lax.fori_loop(0, v_ref.shape[0] // chunk, gate_chunk, 0)

    y = jnp.dot(z_ref[...], wo_ref[...], preferred_element_type=F32)
    o_ref[...] = x_ref[...] + mod_ref[2:3, :] * y


def _gate_proj_residual(uv, g_v, w_s, b_s_t, w_out, xs, mod, *, tm, mod_index):
    rows, d = xs.shape
    width = w_out.shape[0]
    groups, chunk, _ = w_s.shape
    tn = 1024
    return pl.pallas_call(
        functools.partial(_gate_proj_kernel, groups=groups, chunk=chunk),
        grid=(rows // tm, d // tn),
        in_specs=[
            pl.BlockSpec((tm, width), lambda t, j: (t, 0)),
            pl.BlockSpec((tm, width), lambda t, j: (t, 1)),
            pl.BlockSpec((1, width), lambda t, j: (0, 0)),
            pl.BlockSpec(w_s.shape, lambda t, j: (0, 0, 0)),
            pl.BlockSpec(b_s_t.shape, lambda t, j: (0, 0)),
            pl.BlockSpec((width, tn), lambda t, j: (0, j)),
            pl.BlockSpec((tm, tn), lambda t, j: (t, j)),
            pl.BlockSpec((None, N_MOD, tn), lambda t, j: (mod_index(t), 0, j)),
        ],
        out_specs=pl.BlockSpec((tm, tn), lambda t, j: (t, j)),
        out_shape=jax.ShapeDtypeStruct(xs.shape, F32),
        scratch_shapes=[pltpu.VMEM((tm, width), BF16)],
        input_output_aliases={6: 0},
        compiler_params=_params("parallel", "arbitrary"),
        name="gmlp_gate_out",
    )(uv, uv, g_v, w_s, b_s_t, w_out, xs, mod)


def _pool_kernel(x_ref, xp_ref, xn_ref, mod_ref, g_ref, w_ref, sc_ref, o_ref, e_ref,
                 *, tm, lat_tiles, seq, ctx_len):
    t = pl.program_id(0)
    is_lat = t < lat_tiles
    seq_len = jnp.where(is_lat, seq, ctx_len)
    tiles_per_seq = jnp.where(is_lat, seq // tm, ctx_len // tm)
    p = jnp.where(is_lat, t % (seq // tm), (t - lat_tiles) % (ctx_len // tm))
    has_prev = (p > 0).astype(F32)
    has_next = (p < tiles_per_seq - 1).astype(F32)

    g, sh, sc = g_ref[...], mod_ref[0:1, :], mod_ref[1:2, :]
    x = x_ref[...]
    halo = POOL_HALO
    e_ref[0:halo, :] = _normmod(xp_ref[...], g, sh, sc) * has_prev
    e_ref[halo:halo + tm, :] = _normmod(x, g, sh, sc)
    e_ref[halo + tm:2 * halo + tm, :] = _normmod(xn_ref[...], g, sh, sc) * has_next

    pos = p * tm + lax.broadcasted_iota(jnp.int32, (tm, 1), 0)
    n_ext = tm + 2 * halo
    cgw = x.shape[1] // len(POOL_SIZES)
    for gi, w in enumerate(POOL_SIZES):
        cols = slice(gi * cgw, (gi + 1) * cgw)
        e = e_ref[:, cols]
        a = e + pltpu.roll(e, 1, axis=0)
        step = 1
        while 2 * step < w:
            a = pltpu.roll(a, step, axis=0) + pltpu.roll(a, n_ext - step, axis=0)
            step *= 2
        cnt = (jnp.minimum(pos + w // 2, seq_len) - jnp.maximum(pos - w // 2, 0)).astype(F32)
        dlt = a[halo:halo + tm, :] / cnt - e[halo:halo + tm, :]
        y = jnp.dot(dlt.astype(BF16), w_ref[gi], preferred_element_type=F32) * sc_ref[:, cols]
        o_ref[:, cols] = x[:, cols] + mod_ref[2:3, cols] * y


def _pool_residual(xs, mod, g, w_pool, scale, *, tm, lat_rows, seq, ctx_len, mod_index):
    rows, d = xs.shape
    hb = tm // POOL_HALO
    n_hblocks = rows // POOL_HALO
    return pl.pallas_call(
        functools.partial(_pool_kernel, tm=tm, lat_tiles=lat_rows // tm, seq=seq, ctx_len=ctx_len),
        grid=(rows // tm,),
        in_specs=[
            pl.BlockSpec((tm, d), lambda t: (t, 0)),
            pl.BlockSpec((POOL_HALO, d), lambda t: (jnp.maximum(t * hb - 1, 0), 0)),
            pl.BlockSpec((POOL_HALO, d), lambda t: (jnp.minimum((t + 1) * hb, n_hblocks - 1), 0)),
            pl.BlockSpec((None, N_MOD, d), lambda t: (mod_index(t), 0, 0)),
            pl.BlockSpec((1, d), lambda t: (0, 0)),
            pl.BlockSpec(w_pool.shape, lambda t: (0, 0, 0)),
            pl.BlockSpec((1, d), lambda t: (0, 0)),
        ],
        out_specs=pl.BlockSpec((tm, d), lambda t: (t, 0)),
        out_shape=jax.ShapeDtypeStruct(xs.shape, F32),
        scratch_shapes=[pltpu.VMEM((tm + 2 * POOL_HALO, d), F32)],
        compiler_params=_params("parallel"),
        name="pool_mixer",
    )(xs, xs, xs, mod, g, w_pool, scale)


def kernel(x, c, ctx, c_ctx, w_mod, b_mod, g_norm, w_up, w_down, attn_wq, attn_wk, attn_wv, attn_wo, attn_sink, gmlp_w_in, gmlp_g_v, gmlp_w_s, gmlp_b_s, gmlp_w_out, pool_w, pool_scale, g_final):
    B, S, D = x.shape
    L = ctx.shape[1]
    depth = w_mod.shape[0]
    n_q = attn_sink.shape[1]
    hd = attn_wq.shape[2] // n_q
    n_kv = attn_wk.shape[2] // hd
    lat_rows, ctx_rows = B * S, B * L

    tm = min(1024, S, ctx_rows)
    tm_qkv = min(512, S, ctx_rows)
    tm_gate = min(512, S, ctx_rows)
    tm_pool = min(256, S, L)
    assert S % tm == 0 and ctx_rows % tm == 0 and S % ATT_BLOCK == 0 and L % ATT_BLOCK == 0
    assert lat_rows % L == 0 and S // ATT_BLOCK >= 2 and S % GRID_W == 0

    def mod_index_for(tile):
        return lambda t: jnp.minimum(t // (S // tile), B)

    xs = jnp.concatenate([x.reshape(lat_rows, D), ctx.reshape(ctx_rows, D)], axis=0)

    n_cond = -(-(B + 1) // SUBLANE) * SUBLANE
    cond = jnp.concatenate([c, c_ctx[None, :], jnp.zeros((n_cond - B - 1, D), F32)], axis=0)
    mods = _modulation(cond, w_mod, b_mod).reshape(depth, n_cond, N_MOD, D)

    tables = _rope_tables(S, tm_qkv, hd, hd ** -0.5)
    bias = _attn_bias(n_q // n_kv, L)
    qperm, kperm = _head_perm(n_q, hd), _head_perm(n_kv, hd)

    def table_index(t):
        return jnp.where(t < lat_rows // tm_qkv, t % (S // tm_qkv), S // tm_qkv)

    out = None
    for i in range(depth):
        kind, j = i % N_MIXERS, i // N_MIXERS
        last = i == depth - 1
        need_ctx = (not last) or kind == 0
        mod = mods[i]
        g_a, g_m = g_norm[i, 0][None, :], g_norm[i, 1][None, :]
        res_rows = lat_rows if last else lat_rows + ctx_rows

        if kind == 0:
            wqkv = jnp.concatenate([attn_wq[j][:, qperm], attn_wk[j][:, kperm], attn_wv[j]], axis=1).astype(BF16)
            q, k, v = _qkv(xs, mod, g_a, wqkv, tables, tm=tm_qkv, n_q=n_q, n_kv=n_kv, hd=hd,
                           mod_index=mod_index_for(tm_qkv), table_index=table_index)
            o = _attention(q, k, v, attn_sink[j], bias, batch=B, seq=S, ctx_len=L,
                           n_q=n_q, n_kv=n_kv, hd=hd, ctx_out=not last)
            xs = _proj_residual(o, attn_wo[j].astype(BF16), xs, mod, rows=res_rows, tm=tm,
                                mod_index=mod_index_for(tm))
        elif kind == 1:
            uv = _norm_proj_gelu(xs, mod, g_a, gmlp_w_in[j].astype(BF16), tm=tm, mod_index=mod_index_for(tm))
            xs = _gate_proj_residual(uv, gmlp_g_v[j][None, :], gmlp_w_s[j].astype(BF16),
                                     jnp.transpose(gmlp_b_s[j]), gmlp_w_out[j].astype(BF16), xs, mod,
                                     tm=tm_gate, mod_index=mod_index_for(tm_gate))
        else:
            xs = _pool_residual(xs, mod, g_a, pool_w[j].astype(BF16), pool_scale[j][None, :],
                                tm=tm_pool, lat_rows=lat_rows, seq=S, ctx_len=L,
                                mod_index=mod_index_for(tm_pool))

        res = _mlp(xs, mod, g_m, w_up[i].astype(BF16), w_down[i].astype(BF16), g_final[None, :],
                   rows=res_rows, tm=tm, mod_index=mod_index_for(tm), final=last)
        if last:
            out = res
        else:
            xs = res
    return out.reshape(B, S, D)
```

```python
import functools

import jax
import jax.numpy as jnp
from jax import lax
from jax.experimental import pallas as pl
from jax.experimental.pallas import tpu as pltpu

EPS = 1e-6
N_MOD = 6
N_MIXERS = 3
GRID_W = 64
ATT_BLOCK = 128
ROPE_BASE = 10000.0
LOG2_E = 1.4426950408889634
POOL_SIZES = (2, 4, 8, 16)
POOL_HALO = 8
LANE = 128
SUBLANE = 8
VMEM_LIMIT_BYTES = 56 * 1024 * 1024

BF16 = jnp.bfloat16
F32 = jnp.float32


def _params(*sem):
    return pltpu.CompilerParams(dimension_semantics=sem, vmem_limit_bytes=VMEM_LIMIT_BYTES)


def _normmod(x, g, shift, scale):
    ms = jnp.mean(x * x, axis=-1, keepdims=True)
    return (x * lax.rsqrt(ms + EPS)) * g * (1.0 + scale) + shift


ROW_CHUNK = 128
MLP_OUT_CHUNK = 512
MLP_ROW_CHUNK = 256
GMLP_COL_CHUNK = 256


def _for_row_chunks(n_rows, body):
    def step(i, carry):
        body(pl.ds(pl.multiple_of(i * ROW_CHUNK, ROW_CHUNK), ROW_CHUNK))
        return carry
    lax.fori_loop(0, n_rows // ROW_CHUNK, step, 0)


def _normmod_into(dst_ref, x_ref, g_ref, mod_ref, shift_row):
    def body(rows):
        h = _normmod(x_ref[rows, :], g_ref[...], mod_ref[shift_row:shift_row + 1, :],
                     mod_ref[shift_row + 1:shift_row + 2, :])
        dst_ref[rows, :] = h.astype(dst_ref.dtype)
    _for_row_chunks(x_ref.shape[0], body)


def _mod_kernel(c_ref, w_ref, b_ref, o_ref):
    c = c_ref[...]
    s = (c * jax.nn.sigmoid(c)).astype(BF16)
    o_ref[...] = jnp.dot(s, w_ref[...].astype(BF16), preferred_element_type=F32) + b_ref[...]


def _modulation(cond, w_mod, b_mod):
    depth, d, n = w_mod.shape
    r = cond.shape[0]
    tn = 1024
    return pl.pallas_call(
        _mod_kernel,
        grid=(depth, n // tn),
        in_specs=[
            pl.BlockSpec((r, d), lambda i, j: (0, 0)),
            pl.BlockSpec((None, d, tn), lambda i, j: (i, 0, j)),
            pl.BlockSpec((None, 1, tn), lambda i, j: (i, 0, j)),
        ],
        out_specs=pl.BlockSpec((None, r, tn), lambda i, j: (i, 0, j)),
        out_shape=jax.ShapeDtypeStruct((depth, r, n), F32),
        compiler_params=_params("parallel", "parallel"),
        name="modulation",
    )(cond, w_mod, b_mod.reshape(depth, 1, n))


def _mlp_kernel(x_ref, mod_ref, g_ref, wu_ref, wd_ref, gf_ref, o_ref, h_ref, *, final):
    f = pl.program_id(1)
    nf = pl.num_programs(1)
    tm, d = o_ref.shape

    def up(h):
        u = jnp.dot(h, wu_ref[...], preferred_element_type=F32)
        return jnp.square(jnp.maximum(u, 0.0)).astype(BF16)

    @pl.when(f == 0)
    def _first():
        for r0 in range(0, tm, MLP_ROW_CHUNK):
            rows = slice(r0, r0 + MLP_ROW_CHUNK)
            h = _normmod(x_ref[rows, :], g_ref[...], mod_ref[3:4, :], mod_ref[4:5, :]).astype(BF16)
            h_ref[rows, :] = h
            u = up(h)
            for n0 in range(0, d, MLP_OUT_CHUNK):
                cols = slice(n0, n0 + MLP_OUT_CHUNK)
                o_ref[rows, cols] = jnp.dot(u, wd_ref[:, cols], preferred_element_type=F32)

    @pl.when(jnp.logical_and(f > 0, f < nf - 1))
    def _middle():
        u = up(h_ref[...])
        for n0 in range(0, d, MLP_OUT_CHUNK):
            cols = slice(n0, n0 + MLP_OUT_CHUNK)
            o_ref[:, cols] += jnp.dot(u, wd_ref[:, cols], preferred_element_type=F32)

    @pl.when(f == nf - 1)
    def _last():
        for r0 in range(0, tm, MLP_ROW_CHUNK):
            rows = slice(r0, r0 + MLP_ROW_CHUNK)
            u = up(h_ref[rows, :])
            acc = o_ref[rows, :] + jnp.dot(u, wd_ref[...], preferred_element_type=F32)
            y = x_ref[rows, :] + mod_ref[5:6, :] * acc
            if final:
                ms = jnp.mean(y * y, axis=-1, keepdims=True)
                y = (y * lax.rsqrt(ms + EPS)) * gf_ref[...]
            o_ref[rows, :] = y


def _mlp(xs, mod, g, w_up, w_down, g_final, *, layer, rows, tm, mod_index, final):
    _, d, ff = w_up.shape
    tf = 512
    assert ff // tf >= 2 and tm % MLP_ROW_CHUNK == 0
    out_rows = rows if final else xs.shape[0]
    return pl.pallas_call(
        functools.partial(_mlp_kernel, final=final),
        grid=(rows // tm, ff // tf),
        in_specs=[
            pl.BlockSpec((tm, d), lambda t, f: (t, 0)),
            pl.BlockSpec((None, N_MOD, d), lambda t, f: (mod_index(t), 0, 0)),
            pl.BlockSpec((1, d), lambda t, f: (0, 0)),
            pl.BlockSpec((None, d, tf), lambda t, f: (layer, 0, f)),
            pl.BlockSpec((None, tf, d), lambda t, f: (layer, f, 0)),
            pl.BlockSpec((1, d), lambda t, f: (0, 0)),
        ],
        out_specs=pl.BlockSpec((tm, d), lambda t, f: (t, 0)),
        out_shape=jax.ShapeDtypeStruct((out_rows, d), F32),
        scratch_shapes=[pltpu.VMEM((tm, d), BF16)],
        input_output_aliases={} if final else {0: 0},
        compiler_params=_params("parallel", "arbitrary"),
        name="mlp_final" if final else "mlp",
    )(xs, mod, g, w_up, w_down, g_final)


def _qkv_kernel(x_ref, mod_ref, g_ref, w_ref, cq_ref, sq_ref, ck_ref, sk_ref,
                q_ref, k_ref, v_ref, *, n_q, n_kv, hd):
    group = 4
    for r0 in range(0, x_ref.shape[0], MLP_ROW_CHUNK):
        rows = slice(r0, r0 + MLP_ROW_CHUNK)
        h = _normmod(x_ref[rows, :], g_ref[...], mod_ref[0:1, :], mod_ref[1:2, :]).astype(BF16)

        def rope(y, c_ref, s_ref):
            return y * c_ref[rows, :] + pltpu.roll(y, hd // 2, axis=1) * s_ref[rows, :]

        for h0 in range(0, n_q, group):
            acc = jnp.dot(h, w_ref[:, h0 * hd:(h0 + group) * hd], preferred_element_type=F32)
            for j in range(group):
                y = acc[:, j * hd:(j + 1) * hd]
                q_ref[rows, (h0 + j) * hd:(h0 + j + 1) * hd] = rope(y, cq_ref, sq_ref).astype(BF16)
        acc = jnp.dot(h, w_ref[:, n_q * hd:(n_q + n_kv) * hd], preferred_element_type=F32)
        for j in range(n_kv):
            k_ref[rows, j * hd:(j + 1) * hd] = rope(acc[:, j * hd:(j + 1) * hd], ck_ref, sk_ref).astype(BF16)
        acc = jnp.dot(h, w_ref[:, (n_q + n_kv) * hd:(n_q + 2 * n_kv) * hd], preferred_element_type=F32)
        v_ref[rows, :] = acc.astype(BF16)


def _qkv(xs, mod, g, wqkv, tables, *, tm, n_q, n_kv, hd, mod_index, table_index):
    rows, d = xs.shape
    tab_spec = pl.BlockSpec((None, tm, hd), lambda t: (table_index(t), 0, 0))
    return pl.pallas_call(
        functools.partial(_qkv_kernel, n_q=n_q, n_kv=n_kv, hd=hd),
        grid=(rows // tm,),
        in_specs=[
            pl.BlockSpec((tm, d), lambda t: (t, 0)),
            pl.BlockSpec((None, N_MOD, d), lambda t: (mod_index(t), 0, 0)),
            pl.BlockSpec((1, d), lambda t: (0, 0)),
            pl.BlockSpec(wqkv.shape, lambda t: (0, 0), pipeline_mode=pl.Buffered(1)),
            tab_spec, tab_spec, tab_spec, tab_spec,
        ],
        out_specs=[
            pl.BlockSpec((tm, n_q * hd), lambda t: (t, 0)),
            pl.BlockSpec((tm, n_kv * hd), lambda t: (t, 0)),
            pl.BlockSpec((tm, n_kv * hd), lambda t: (t, 0)),
        ],
        out_shape=[
            jax.ShapeDtypeStruct((rows, n_q * hd), BF16),
            jax.ShapeDtypeStruct((rows, n_kv * hd), BF16),
            jax.ShapeDtypeStruct((rows, n_kv * hd), BF16),
        ],
        compiler_params=_params("parallel"),
        name="qkv_rope",
    )(xs, mod, g, wqkv, *tables)


def _attn_kernel(sink_ref, q_ref, kp_ref, kc_ref, kn_ref, kx_ref, vp_ref, vc_ref, vn_ref, vx_ref,
                 bias_ref, o_ref, *, n_kv, grp, hd):
    bias = bias_ref[...]
    blk = q_ref.shape[0]
    n_keys = bias.shape[1]
    ones = jnp.ones((n_keys, hd), BF16)
    for hk in range(n_kv):
        sl = slice(hk * hd, (hk + 1) * hd)
        k_all = jnp.concatenate([kp_ref[:, sl], kc_ref[:, sl], kn_ref[:, sl], kx_ref[:, sl]], axis=0)
        v_all = jnp.concatenate([vp_ref[:, sl], vc_ref[:, sl], vn_ref[:, sl], vx_ref[:, sl]], axis=0)
        v_aug = jnp.concatenate([v_all, ones], axis=1)
        heads = [hk * grp + j for j in range(grp)]
        qh = jnp.concatenate([q_ref[:, hq * hd:(hq + 1) * hd] for hq in heads], axis=0)
        s = lax.dot_general(qh, k_all, (((1,), (1,)), ((), ())), preferred_element_type=F32) + bias
        sink = jnp.concatenate([jnp.full((blk, 1), sink_ref[hq], F32) for hq in heads], axis=0)
        m = jnp.maximum(jnp.max(s, axis=-1, keepdims=True), sink)
        p = jnp.exp2(s - m).astype(BF16)
        ov = jnp.dot(p, v_aug, preferred_element_type=F32)
        o = ov[:, :hd] / (ov[:, hd:] + jnp.exp2(sink - m))
        for j, hq in enumerate(heads):
            o_ref[:, hq * hd:(hq + 1) * hd] = o[j * blk:(j + 1) * blk, :].astype(o_ref.dtype)


def _attention(q, k, v, sink, bias, *, batch, seq, ctx_len, n_q, n_kv, hd, ctx_out):
    rows = q.shape[0]
    blk = ATT_BLOCK
    nb = seq // blk
    ncb = ctx_len // blk
    lat_blocks = batch * nb
    nq_blocks = nb + (ncb if ctx_out else 0)

    def q_idx(b, n):
        return (jnp.where(n < nb, b * nb + n, lat_blocks + b * ncb + (n - nb)), 0)

    def kv_idx(off):
        return lambda b, n: (b * nb + jnp.clip(n + off, 0, nb - 1), 0)

    def ctx_idx(b, n):
        return (lat_blocks * blk // ctx_len + b, 0)

    def bias_idx(b, n):
        return (jnp.where(n >= nb, 3, jnp.where(n == 0, 0, jnp.where(n == nb - 1, 2, 1))), 0, 0)

    kv_w = n_kv * hd
    kv_specs = [pl.BlockSpec((blk, kv_w), kv_idx(-1)), pl.BlockSpec((blk, kv_w), kv_idx(0)),
                pl.BlockSpec((blk, kv_w), kv_idx(1)), pl.BlockSpec((ctx_len, kv_w), ctx_idx)]
    return pl.pallas_call(
        functools.partial(_attn_kernel, n_kv=n_kv, grp=n_q // n_kv, hd=hd),
        grid=(batch, nq_blocks),
        in_specs=[pl.BlockSpec(memory_space=pltpu.SMEM),
                  pl.BlockSpec((blk, n_q * hd), q_idx)] + kv_specs + kv_specs
                 + [pl.BlockSpec((None,) + bias.shape[1:], bias_idx)],
        out_specs=pl.BlockSpec((blk, n_q * hd), q_idx),
        out_shape=jax.ShapeDtypeStruct((rows, n_q * hd), BF16),
        compiler_params=_params("parallel", "parallel"),
        name="window_attention",
    )(sink, q, k, k, k, k, v, v, v, v, bias)


def _proj_res_kernel(a_ref, w_ref, x_ref, mod_ref, o_ref):
    a = a_ref[...]
    for n0 in range(0, o_ref.shape[1], MLP_OUT_CHUNK):
        cols = slice(n0, n0 + MLP_OUT_CHUNK)
        y = jnp.dot(a, w_ref[:, cols], preferred_element_type=F32)
        o_ref[:, cols] = x_ref[:, cols] + mod_ref[2:3, cols] * y


def _proj_residual(a, w, xs, mod, *, layer, rows, tm, mod_index):
    _, kdim, d = w.shape
    return pl.pallas_call(
        _proj_res_kernel,
        grid=(rows // tm,),
        in_specs=[
            pl.BlockSpec((tm, kdim), lambda t: (t, 0)),
            pl.BlockSpec((None, kdim, d), lambda t: (layer, 0, 0), pipeline_mode=pl.Buffered(1)),
            pl.BlockSpec((tm, d), lambda t: (t, 0)),
            pl.BlockSpec((None, N_MOD, d), lambda t: (mod_index(t), 0, 0)),
        ],
        out_specs=pl.BlockSpec((tm, d), lambda t: (t, 0)),
        out_shape=jax.ShapeDtypeStruct(xs.shape, F32),
        input_output_aliases={2: 0},
        compiler_params=_params("parallel"),
        name="proj_residual",
    )(a, w, xs, mod)


def _rope_tables(seq, tm, hd, scale):
    q = hd // 4
    pos = jnp.arange(seq)
    row = (pos // GRID_W).astype(F32)[:, None]
    col = (pos % GRID_W).astype(F32)[:, None]
    axis_dim = hd // 2
    inv_freq = ROPE_BASE ** (-jnp.arange(0, axis_dim, 2, dtype=F32) / axis_dim)
    a_row, a_col = row * inv_freq, col * inv_freq
    cos = jnp.concatenate([jnp.cos(a_row), jnp.cos(a_col)] * 2, axis=1)
    sin = jnp.concatenate([-jnp.sin(a_row), -jnp.sin(a_col), jnp.sin(a_row), jnp.sin(a_col)], axis=1)
    assert cos.shape == (seq, 4 * q)
    cos = jnp.concatenate([cos.reshape(seq // tm, tm, hd), jnp.ones((1, tm, hd), F32)], axis=0)
    sin = jnp.concatenate([sin.reshape(seq // tm, tm, hd), jnp.zeros((1, tm, hd), F32)], axis=0)
    return cos * scale, sin * scale, cos, sin


def _head_perm(n_heads, hd):
    q = hd // 4
    one = jnp.concatenate([jnp.arange(0, q), jnp.arange(2 * q, 3 * q), jnp.arange(q, 2 * q), jnp.arange(3 * q, 4 * q)])
    return (jnp.arange(n_heads)[:, None] * hd + one[None, :]).reshape(-1)


def _attn_bias(grp, ctx_len):
    blk = ATT_BLOCK
    qi = jnp.arange(grp * blk)[:, None] % blk
    kj = jnp.arange(3 * blk + ctx_len)[None, :]
    is_ctx = kj >= 3 * blk
    win = jnp.abs(kj - blk - qi) <= ATT_BLOCK
    variants = [win & (kj >= blk), win, win & (kj < 2 * blk), jnp.zeros_like(win)]
    return jnp.stack([jnp.where(is_ctx | m, 0.0, -jnp.inf).astype(F32) for m in variants])


def _gmlp_gate_kernel(x_ref, mod_ref, g_ref, w_ref, gv_ref, ws_ref, bs_ref, z_ref,
                      h_ref, v_ref, mu_ref, rs_ref, *, n_vt, chunk, cg):
    j = pl.program_id(1)
    tm, tn = z_ref.shape

    @pl.when(j == 0)
    def _():
        _normmod_into(h_ref, x_ref, g_ref, mod_ref, 0)

    def gelu_proj(cols):
        y = jnp.dot(h_ref[...], w_ref[:, cols], preferred_element_type=F32)
        return jax.nn.gelu(y)

    @pl.when(j < n_vt)
    def _v_tile():
        for n0 in range(0, tn, GMLP_COL_CHUNK):
            cols = slice(n0, n0 + GMLP_COL_CHUNK)
            v_ref[j, :, cols] = gelu_proj(cols).astype(BF16)

    @pl.when(j == n_vt)
    def _v_stats():
        width = n_vt * tn

        def body(rows):
            tot = jnp.zeros((ROW_CHUNK, 1), F32)
            for k in range(n_vt):
                tot += jnp.sum(v_ref[k, rows, :].astype(F32), axis=-1, keepdims=True)
            mu = tot / width
            sq = jnp.zeros((ROW_CHUNK, 1), F32)
            for k in range(n_vt):
                dv = v_ref[k, rows, :].astype(F32) - mu
                sq += jnp.sum(dv * dv, axis=-1, keepdims=True)
            mu_ref[rows, :] = jnp.broadcast_to(mu, (ROW_CHUNK, LANE))
            rs_ref[rows, :] = jnp.broadcast_to(lax.rsqrt(sq / width + EPS), (ROW_CHUNK, LANE))
        _for_row_chunks(tm, body)

    @pl.when(j >= n_vt)
    def _u_tile():
        jj = j - n_vt
        rep = cg // LANE
        for n0 in range(0, tn, GMLP_COL_CHUNK):
            u = gelu_proj(slice(n0, n0 + GMLP_COL_CHUNK))
            for g0 in range(0, GMLP_COL_CHUNK, cg):
                gi = jj * (tn // cg) + (n0 + g0) // cg
                cols = slice(n0 + g0, n0 + g0 + cg)
                w_s = ws_ref[gi]
                b_s = jnp.concatenate([bs_ref[gi]] * rep, axis=1)
                gv = gv_ref[jj, :, cols]
                for r0 in range(0, tm, chunk):
                    rows = slice(r0, r0 + chunk)
                    mu = jnp.concatenate([mu_ref[rows, :]] * rep, axis=1)
                    rs = jnp.concatenate([rs_ref[rows, :]] * rep, axis=1)
                    vn = ((v_ref[jj, rows, cols].astype(F32) - mu) * rs * gv).astype(BF16)
                    sv = jnp.dot(w_s, vn, preferred_element_type=F32) + b_s
                    z_ref[rows, cols] = (u[rows, g0:g0 + cg] * sv).astype(BF16)


def _gmlp_gate(xs, mod, g, w_in, g_v, w_s, b_s, *, layer, tm, mod_index):
    rows, d = xs.shape
    width = w_in.shape[2] // 2
    groups, chunk = w_s.shape[1], w_s.shape[2]
    cg = width // groups
    tn = 1024
    n_vt = width // tn
    assert tn % GMLP_COL_CHUNK == 0 and GMLP_COL_CHUNK % cg == 0 and cg % LANE == 0 and tm % chunk == 0
    g_v_tiles = g_v[layer].reshape(n_vt, 1, tn)
    b_s_wide = jnp.broadcast_to(b_s[layer][:, :, None], (groups, chunk, LANE))
    return pl.pallas_call(
        functools.partial(_gmlp_gate_kernel, n_vt=n_vt, chunk=chunk, cg=cg),
        grid=(rows // tm, 2 * n_vt),
        in_specs=[
            pl.BlockSpec((tm, d), lambda t, j: (t, 0)),
            pl.BlockSpec((None, N_MOD, d), lambda t, j: (mod_index(t), 0, 0)),
            pl.BlockSpec((1, d), lambda t, j: (0, 0)),
            pl.BlockSpec((None, d, tn), lambda t, j: (layer, 0, (j + n_vt) % (2 * n_vt))),
            pl.BlockSpec((n_vt, 1, tn), lambda t, j: (0, 0, 0)),
            pl.BlockSpec((None, groups, chunk, chunk), lambda t, j: (layer, 0, 0, 0)),
            pl.BlockSpec((groups, chunk, LANE), lambda t, j: (0, 0, 0)),
        ],
        out_specs=pl.BlockSpec((tm, tn), lambda t, j: (t, jnp.maximum(j - n_vt, 0))),
        out_shape=jax.ShapeDtypeStruct((rows, width), BF16),
        scratch_shapes=[pltpu.VMEM((tm, d), BF16), pltpu.VMEM((n_vt, tm, tn), BF16),
                        pltpu.VMEM((tm, LANE), F32), pltpu.VMEM((tm, LANE), F32)],
        compiler_params=_params("parallel", "arbitrary"),
        name="gmlp_gate",
    )(xs, mod, g, w_in, g_v_tiles, w_s, b_s_wide)


def _pool_kernel(x_ref, xp_ref, xn_ref, mod_ref, g_ref, w_ref, sc_ref, o_ref, e_ref,
                 *, tm, lat_tiles, seq, ctx_len):
    t = pl.program_id(0)
    is_lat = t < lat_tiles
    seq_len = jnp.where(is_lat, seq, ctx_len)
    tiles_per_seq = jnp.where(is_lat, seq // tm, ctx_len // tm)
    p = jnp.where(is_lat, t % (seq // tm), (t - lat_tiles) % (ctx_len // tm))
    has_prev = (p > 0).astype(F32)
    has_next = (p < tiles_per_seq - 1).astype(F32)

    g, sh, sc = g_ref[...], mod_ref[0:1, :], mod_ref[1:2, :]
    x = x_ref[...]
    halo = POOL_HALO
    e_ref[0:halo, :] = _normmod(xp_ref[...], g, sh, sc) * has_prev
    e_ref[halo:halo + tm, :] = _normmod(x, g, sh, sc)
    e_ref[halo + tm:2 * halo + tm, :] = _normmod(xn_ref[...], g, sh, sc) * has_next

    pos = p * tm + lax.broadcasted_iota(jnp.int32, (tm, 1), 0)
    n_ext = tm + 2 * halo
    cgw = x.shape[1] // len(POOL_SIZES)
    for gi, w in enumerate(POOL_SIZES):
        cols = slice(gi * cgw, (gi + 1) * cgw)
        e = e_ref[:, cols]
        a = e + pltpu.roll(e, 1, axis=0)
        step = 1
        while 2 * step < w:
            a = pltpu.roll(a, step, axis=0) + pltpu.roll(a, n_ext - step, axis=0)
            step *= 2
        cnt = (jnp.minimum(pos + w // 2, seq_len) - jnp.maximum(pos - w // 2, 0)).astype(F32)
        dlt = a[halo:halo + tm, :] / cnt - e[halo:halo + tm, :]
        y = jnp.dot(dlt.astype(BF16), w_ref[gi], preferred_element_type=F32) * sc_ref[:, cols]
        o_ref[:, cols] = x[:, cols] + mod_ref[2:3, cols] * y


def _pool_residual(xs, mod, g, w_pool, scale, *, tm, lat_rows, seq, ctx_len, mod_index):
    rows, d = xs.shape
    hb = tm // POOL_HALO
    n_hblocks = rows // POOL_HALO
    return pl.pallas_call(
        functools.partial(_pool_kernel, tm=tm, lat_tiles=lat_rows // tm, seq=seq, ctx_len=ctx_len),
        grid=(rows // tm,),
        in_specs=[
            pl.BlockSpec((tm, d), lambda t: (t, 0)),
            pl.BlockSpec((POOL_HALO, d), lambda t: (jnp.maximum(t * hb - 1, 0), 0)),
            pl.BlockSpec((POOL_HALO, d), lambda t: (jnp.minimum((t + 1) * hb, n_hblocks - 1), 0)),
            pl.BlockSpec((None, N_MOD, d), lambda t: (mod_index(t), 0, 0)),
            pl.BlockSpec((1, d), lambda t: (0, 0)),
            pl.BlockSpec(w_pool.shape, lambda t: (0, 0, 0)),
            pl.BlockSpec((1, d), lambda t: (0, 0)),
        ],
        out_specs=pl.BlockSpec((tm, d), lambda t: (t, 0)),
        out_shape=jax.ShapeDtypeStruct(xs.shape, F32),
        scratch_shapes=[pltpu.VMEM((tm + 2 * POOL_HALO, d), F32)],
        compiler_params=_params("parallel"),
        name="pool_mixer",
    )(xs, xs, xs, mod, g, w_pool, scale)


def kernel(x, c, ctx, c_ctx, w_mod, b_mod, g_norm, w_up, w_down, attn_wq, attn_wk, attn_wv, attn_wo, attn_sink, gmlp_w_in, gmlp_g_v, gmlp_w_s, gmlp_b_s, gmlp_w_out, pool_w, pool_scale, g_final):
    B, S, D = x.shape
    L = ctx.shape[1]
    depth = w_mod.shape[0]
    n_q = attn_sink.shape[1]
    hd = attn_wq.shape[2] // n_q
    n_kv = attn_wk.shape[2] // hd
    lat_rows, ctx_rows = B * S, B * L

    tm = min(1024, S, ctx_rows)
    tm_qkv = min(1024, S, ctx_rows)
    tm_proj = min(512, S, ctx_rows)
    tm_pool = min(256, S, L)
    assert S % tm == 0 and ctx_rows % tm == 0 and S % ATT_BLOCK == 0 and L % ATT_BLOCK == 0
    assert lat_rows % L == 0 and S // ATT_BLOCK >= 2 and S % GRID_W == 0

    def mod_index_for(tile):
        return lambda t: jnp.minimum(t // (S // tile), B)

    xs = jnp.concatenate([x.reshape(lat_rows, D), ctx.reshape(ctx_rows, D)], axis=0)

    n_cond = -(-(B + 1) // SUBLANE) * SUBLANE
    cond = jnp.concatenate([c, c_ctx[None, :], jnp.zeros((n_cond - B - 1, D), F32)], axis=0)
    mods = _modulation(cond, w_mod, b_mod).reshape(depth, n_cond, N_MOD, D)

    tables = _rope_tables(S, tm_qkv, hd, hd ** -0.5 * LOG2_E)
    bias = _attn_bias(n_q // n_kv, L)
    qperm, kperm = _head_perm(n_q, hd), _head_perm(n_kv, hd)

    def table_index(t):
        return jnp.where(t < lat_rows // tm_qkv, t % (S // tm_qkv), S // tm_qkv)

    w_up_bf, w_down_bf = w_up.astype(BF16), w_down.astype(BF16)
    attn_wo_bf = attn_wo.astype(BF16)
    gmlp_w_in_bf, gmlp_w_s_bf, gmlp_w_out_bf = gmlp_w_in.astype(BF16), gmlp_w_s.astype(BF16), gmlp_w_out.astype(BF16)

    out = None
    for i in range(depth):
        kind, j = i % N_MIXERS, i // N_MIXERS
        last = i == depth - 1
        mod = mods[i]
        g_a, g_m = g_norm[i, 0][None, :], g_norm[i, 1][None, :]
        res_rows = lat_rows if last else lat_rows + ctx_rows

        if kind == 0:
            wqkv = jnp.concatenate([attn_wq[j][:, qperm], attn_wk[j][:, kperm], attn_wv[j]], axis=1).astype(BF16)
            q, k, v = _qkv(xs, mod, g_a, wqkv, tables, tm=tm_qkv, n_q=n_q, n_kv=n_kv, hd=hd,
                           mod_index=mod_index_for(tm_qkv), table_index=table_index)
            o = _attention(q, k, v, attn_sink[j] * LOG2_E, bias, batch=B, seq=S, ctx_len=L,
                           n_q=n_q, n_kv=n_kv, hd=hd, ctx_out=not last)
            xs = _proj_residual(o, attn_wo_bf, xs, mod, layer=j, rows=res_rows, tm=tm_proj,
                                mod_index=mod_index_for(tm_proj))
        elif kind == 1:
            z = _gmlp_gate(xs, mod, g_a, gmlp_w_in_bf, gmlp_g_v, gmlp_w_s_bf, gmlp_b_s,
                           layer=j, tm=tm, mod_index=mod_index_for(tm))
            xs = _proj_residual(z, gmlp_w_out_bf, xs, mod, layer=j, rows=res_rows, tm=tm_proj,
                                mod_index=mod_index_for(tm_proj))
        else:
            xs = _pool_residual(xs, mod, g_a, pool_w[j].astype(BF16), pool_scale[j][None, :],
                                tm=tm_pool, lat_rows=lat_rows, seq=S, ctx_len=L,
                                mod_index=mod_index_for(tm_pool))

        res = _mlp(xs, mod, g_m, w_up_bf, w_down_bf, g_final[None, :], layer=i, rows=res_rows, tm=tm, mod_index=mod_index_for(tm), final=last)
        if last:
            out = res
        else:
            xs = res
    return out.reshape(B, S, D)
```

```python
import functools

import jax
import jax.numpy as jnp
from jax import lax
from jax.experimental import pallas as pl
from jax.experimental.pallas import tpu as pltpu

EPS = 1e-6
N_MOD = 6
N_MIXERS = 3
GRID_W = 64
ATT_BLOCK = 128
ROPE_BASE = 10000.0
LOG2_E = 1.4426950408889634
POOL_SIZES = (2, 4, 8, 16)
POOL_HALO = 8
LANE = 128
SUBLANE = 8
VMEM_LIMIT_BYTES = 56 * 1024 * 1024

BF16 = jnp.bfloat16
F32 = jnp.float32


def _params(*sem):
    return pltpu.CompilerParams(dimension_semantics=sem, vmem_limit_bytes=VMEM_LIMIT_BYTES)


def _normmod(x, g, shift, scale):
    ms = jnp.mean(x * x, axis=-1, keepdims=True)
    return (x * lax.rsqrt(ms + EPS)) * g * (1.0 + scale) + shift


ROW_CHUNK = 128
MLP_OUT_CHUNK = 512
MLP_ROW_CHUNK = 256
GMLP_COL_CHUNK = 256
MLP_FF_TILE = 512
GMLP_IN_TILE = 1024


def _for_row_chunks(n_rows, body):
    def step(i, carry):
        body(pl.ds(pl.multiple_of(i * ROW_CHUNK, ROW_CHUNK), ROW_CHUNK))
        return carry
    lax.fori_loop(0, n_rows // ROW_CHUNK, step, 0)


def _normmod_into(dst_ref, x_ref, g_ref, mod_ref, shift_row):
    def body(rows):
        h = _normmod(x_ref[rows, :], g_ref[...], mod_ref[shift_row:shift_row + 1, :],
                     mod_ref[shift_row + 1:shift_row + 2, :])
        dst_ref[rows, :] = h.astype(dst_ref.dtype)
    _for_row_chunks(x_ref.shape[0], body)


def _mod_kernel(c_ref, w_ref, b_ref, o_ref):
    c = c_ref[...]
    s = (c * jax.nn.sigmoid(c)).astype(BF16)
    o_ref[...] = jnp.dot(s, w_ref[...].astype(BF16), preferred_element_type=F32) + b_ref[...]


def _modulation(cond, w_mod, b_mod):
    depth, d, n = w_mod.shape
    r = cond.shape[0]
    tn = 1024
    return pl.pallas_call(
        _mod_kernel,
        grid=(depth, n // tn),
        in_specs=[
            pl.BlockSpec((r, d), lambda i, j: (0, 0)),
            pl.BlockSpec((None, d, tn), lambda i, j: (i, 0, j)),
            pl.BlockSpec((None, 1, tn), lambda i, j: (i, 0, j)),
        ],
        out_specs=pl.BlockSpec((None, r, tn), lambda i, j: (i, 0, j)),
        out_shape=jax.ShapeDtypeStruct((depth, r, n), F32),
        compiler_params=_params("parallel", "parallel"),
        name="modulation",
    )(cond, w_mod, b_mod.reshape(depth, 1, n))


def _mlp_kernel(x_ref, mod_ref, g_ref, wu_ref, wd_ref, gf_ref, o_ref, h_ref, *, final):
    f = pl.program_id(1)
    nf = pl.num_programs(1)
    tm, d = o_ref.shape

    def up(h):
        u = jnp.dot(h, wu_ref[...], preferred_element_type=F32)
        return jnp.square(jnp.maximum(u, 0.0)).astype(BF16)

    @pl.when(f == 0)
    def _first():
        for r0 in range(0, tm, MLP_ROW_CHUNK):
            rows = slice(r0, r0 + MLP_ROW_CHUNK)
            h = _normmod(x_ref[rows, :], g_ref[...], mod_ref[3:4, :], mod_ref[4:5, :]).astype(BF16)
            h_ref[rows, :] = h
            u = up(h)
            for n0 in range(0, d, MLP_OUT_CHUNK):
                cols = slice(n0, n0 + MLP_OUT_CHUNK)
                o_ref[rows, cols] = jnp.dot(u, wd_ref[:, cols], preferred_element_type=F32)

    @pl.when(jnp.logical_and(f > 0, f < nf - 1))
    def _middle():
        u = up(h_ref[...])
        for n0 in range(0, d, MLP_OUT_CHUNK):
            cols = slice(n0, n0 + MLP_OUT_CHUNK)
            o_ref[:, cols] += jnp.dot(u, wd_ref[:, cols], preferred_element_type=F32)

    @pl.when(f == nf - 1)
    def _last():
        for r0 in range(0, tm, MLP_ROW_CHUNK):
            rows = slice(r0, r0 + MLP_ROW_CHUNK)
            u = up(h_ref[rows, :])
            acc = o_ref[rows, :] + jnp.dot(u, wd_ref[...], preferred_element_type=F32)
            y = x_ref[rows, :] + mod_ref[5:6, :] * acc
            if final:
                ms = jnp.mean(y * y, axis=-1, keepdims=True)
                y = (y * lax.rsqrt(ms + EPS)) * gf_ref[...]
            o_ref[rows, :] = y


def _mlp(xs, mod, g, w_up, w_down, g_final, *, layer, rows, tm, mod_index, final):
    _, nf, d, tf = w_up.shape
    ff = nf * tf
    assert nf >= 2 and tm % MLP_ROW_CHUNK == 0
    out_rows = rows if final else xs.shape[0]
    return pl.pallas_call(
        functools.partial(_mlp_kernel, final=final),
        grid=(rows // tm, ff // tf),
        in_specs=[
            pl.BlockSpec((tm, d), lambda t, f: (t, 0)),
            pl.BlockSpec((None, N_MOD, d), lambda t, f: (mod_index(t), 0, 0)),
            pl.BlockSpec((1, d), lambda t, f: (0, 0)),
            pl.BlockSpec((None, None, d, tf), lambda t, f: (layer, f, 0, 0)),
            pl.BlockSpec((None, tf, d), lambda t, f: (layer, f, 0)),
            pl.BlockSpec((1, d), lambda t, f: (0, 0)),
        ],
        out_specs=pl.BlockSpec((tm, d), lambda t, f: (t, 0)),
        out_shape=jax.ShapeDtypeStruct((out_rows, d), F32),
        scratch_shapes=[pltpu.VMEM((tm, d), BF16)],
        input_output_aliases={} if final else {0: 0},
        compiler_params=_params("parallel", "arbitrary"),
        name="mlp_final" if final else "mlp",
    )(xs, mod, g, w_up, w_down, g_final)


def _qkv_kernel(*refs, n_q, n_kv, hd, lat_tiles):
    if lat_tiles is None:
        x_ref, mod_ref, g_ref, w_ref, cq_ref, sq_ref, ck_ref, sk_ref, q_ref, k_ref, v_ref = refs
    else:
        (xl_ref, xc_ref, mod_ref, g_ref, w_ref, cq_ref, sq_ref, ck_ref, sk_ref,
         q_ref, k_ref, v_ref, xs_ref) = refs
        is_lat = pl.program_id(0) < lat_tiles
    group = 4
    for r0 in range(0, q_ref.shape[0], MLP_ROW_CHUNK):
        rows = slice(r0, r0 + MLP_ROW_CHUNK)
        if lat_tiles is None:
            x = x_ref[rows, :]
        else:
            x = jnp.where(is_lat, xl_ref[rows, :], xc_ref[rows, :])
            xs_ref[rows, :] = x
        h = _normmod(x, g_ref[...], mod_ref[0:1, :], mod_ref[1:2, :]).astype(BF16)

        def rope(y, c_ref, s_ref):
            return y * c_ref[rows, :] + pltpu.roll(y, hd // 2, axis=1) * s_ref[rows, :]

        for h0 in range(0, n_q, group):
            acc = jnp.dot(h, w_ref[:, h0 * hd:(h0 + group) * hd], preferred_element_type=F32)
            for j in range(group):
                y = acc[:, j * hd:(j + 1) * hd]
                q_ref[rows, (h0 + j) * hd:(h0 + j + 1) * hd] = rope(y, cq_ref, sq_ref).astype(BF16)
        acc = jnp.dot(h, w_ref[:, n_q * hd:(n_q + n_kv) * hd], preferred_element_type=F32)
        for j in range(n_kv):
            k_ref[rows, j * hd:(j + 1) * hd] = rope(acc[:, j * hd:(j + 1) * hd], ck_ref, sk_ref).astype(BF16)
        acc = jnp.dot(h, w_ref[:, (n_q + n_kv) * hd:(n_q + 2 * n_kv) * hd], preferred_element_type=F32)
        v_ref[rows, :] = acc.astype(BF16)


def _qkv(streams, mod, g, wqkv, tables, *, tm, n_q, n_kv, hd, mod_index, table_index):
    split = len(streams) == 2
    rows = sum(s.shape[0] for s in streams)
    d = streams[0].shape[1]
    lat_tiles = streams[0].shape[0] // tm if split else None
    tab_spec = pl.BlockSpec((None, tm, hd), lambda t: (table_index(t), 0, 0))
    if split:
        x_specs = [pl.BlockSpec((tm, d), lambda t: (jnp.minimum(t, lat_tiles - 1), 0)),
                   pl.BlockSpec((tm, d), lambda t: (jnp.maximum(t - lat_tiles, 0), 0))]
    else:
        x_specs = [pl.BlockSpec((tm, d), lambda t: (t, 0))]
    out_specs = [
        pl.BlockSpec((tm, n_q * hd), lambda t: (t, 0)),
        pl.BlockSpec((tm, n_kv * hd), lambda t: (t, 0)),
        pl.BlockSpec((tm, n_kv * hd), lambda t: (t, 0)),
    ]
    out_shape = [
        jax.ShapeDtypeStruct((rows, n_q * hd), BF16),
        jax.ShapeDtypeStruct((rows, n_kv * hd), BF16),
        jax.ShapeDtypeStruct((rows, n_kv * hd), BF16),
    ]
    if split:
        out_specs.append(pl.BlockSpec((tm, d), lambda t: (t, 0)))
        out_shape.append(jax.ShapeDtypeStruct((rows, d), F32))
    return pl.pallas_call(
        functools.partial(_qkv_kernel, n_q=n_q, n_kv=n_kv, hd=hd, lat_tiles=lat_tiles),
        grid=(rows // tm,),
        in_specs=x_specs + [
            pl.BlockSpec((None, N_MOD, d), lambda t: (mod_index(t), 0, 0)),
            pl.BlockSpec((1, d), lambda t: (0, 0)),
            pl.BlockSpec(wqkv.shape, lambda t: (0, 0), pipeline_mode=pl.Buffered(1)),
            tab_spec, tab_spec, tab_spec, tab_spec,
        ],
        out_specs=out_specs,
        out_shape=out_shape,
        compiler_params=_params("parallel"),
        name="qkv_rope_join" if split else "qkv_rope",
    )(*streams, mod, g, wqkv, *tables)


def _attn_kernel(sink_ref, q_ref, kp_ref, kc_ref, kn_ref, kx_ref, vp_ref, vc_ref, vn_ref, vx_ref,
                 bias_ref, o_ref, *, n_kv, grp, hd):
    bias = bias_ref[...]
    blk = q_ref.shape[0]
    n_keys = bias.shape[1]
    ones = jnp.ones((n_keys, hd), BF16)
    for hk in range(n_kv):
        sl = slice(hk * hd, (hk + 1) * hd)
        k_all = jnp.concatenate([kp_ref[:, sl], kc_ref[:, sl], kn_ref[:, sl], kx_ref[:, sl]], axis=0)
        v_all = jnp.concatenate([vp_ref[:, sl], vc_ref[:, sl], vn_ref[:, sl], vx_ref[:, sl]], axis=0)
        v_aug = jnp.concatenate([v_all, ones], axis=1)
        heads = [hk * grp + j for j in range(grp)]
        qh = jnp.concatenate([q_ref[:, hq * hd:(hq + 1) * hd] for hq in heads], axis=0)
        s = lax.dot_general(qh, k_all, (((1,), (1,)), ((), ())), preferred_element_type=F32) + bias
        sink = jnp.concatenate([jnp.full((blk, 1), sink_ref[hq], F32) for hq in heads], axis=0)
        m = jnp.maximum(jnp.max(s, axis=-1, keepdims=True), sink)
        p = jnp.exp2(s - m).astype(BF16)
        ov = jnp.dot(p, v_aug, preferred_element_type=F32)
        o = ov[:, :hd] / (ov[:, hd:] + jnp.exp2(sink - m))
        for j, hq in enumerate(heads):
            o_ref[:, hq * hd:(hq + 1) * hd] = o[j * blk:(j + 1) * blk, :].astype(o_ref.dtype)


def _attention(q, k, v, sink, bias, *, batch, seq, ctx_len, n_q, n_kv, hd, ctx_out):
    rows = q.shape[0]
    blk = ATT_BLOCK
    nb = seq // blk
    ncb = ctx_len // blk
    lat_blocks = batch * nb
    nq_blocks = nb + (ncb if ctx_out else 0)

    def q_idx(b, n):
        return (jnp.where(n < nb, b * nb + n, lat_blocks + b * ncb + (n - nb)), 0)

    def kv_idx(off):
        return lambda b, n: (b * nb + jnp.clip(n + off, 0, nb - 1), 0)

    def ctx_idx(b, n):
        return (lat_blocks * blk // ctx_len + b, 0)

    def bias_idx(b, n):
        return (jnp.where(n >= nb, 3, jnp.where(n == 0, 0, jnp.where(n == nb - 1, 2, 1))), 0, 0)

    kv_w = n_kv * hd
    kv_specs = [pl.BlockSpec((blk, kv_w), kv_idx(-1)), pl.BlockSpec((blk, kv_w), kv_idx(0)),
                pl.BlockSpec((blk, kv_w), kv_idx(1)), pl.BlockSpec((ctx_len, kv_w), ctx_idx)]
    return pl.pallas_call(
        functools.partial(_attn_kernel, n_kv=n_kv, grp=n_q // n_kv, hd=hd),
        grid=(batch, nq_blocks),
        in_specs=[pl.BlockSpec(memory_space=pltpu.SMEM),
                  pl.BlockSpec((blk, n_q * hd), q_idx)] + kv_specs + kv_specs
                 + [pl.BlockSpec((None,) + bias.shape[1:], bias_idx)],
        out_specs=pl.BlockSpec((blk, n_q * hd), q_idx),
        out_shape=jax.ShapeDtypeStruct((rows, n_q * hd), BF16),
        compiler_params=_params("parallel", "parallel"),
        name="window_attention",
    )(sink, q, k, k, k, k, v, v, v, v, bias)


def _proj_res_kernel(a_ref, w_ref, x_ref, mod_ref, o_ref):
    a = a_ref[...]
    for n0 in range(0, o_ref.shape[1], MLP_OUT_CHUNK):
        cols = slice(n0, n0 + MLP_OUT_CHUNK)
        y = jnp.dot(a, w_ref[:, cols], preferred_element_type=F32)
        o_ref[:, cols] = x_ref[:, cols] + mod_ref[2:3, cols] * y


def _proj_residual(a, w, xs, mod, *, layer, rows, tm, mod_index):
    _, kdim, d = w.shape
    return pl.pallas_call(
        _proj_res_kernel,
        grid=(rows // tm,),
        in_specs=[
            pl.BlockSpec((tm, kdim), lambda t: (t, 0)),
            pl.BlockSpec((None, kdim, d), lambda t: (layer, 0, 0), pipeline_mode=pl.Buffered(1)),
            pl.BlockSpec((tm, d), lambda t: (t, 0)),
            pl.BlockSpec((None, N_MOD, d), lambda t: (mod_index(t), 0, 0)),
        ],
        out_specs=pl.BlockSpec((tm, d), lambda t: (t, 0)),
        out_shape=jax.ShapeDtypeStruct(xs.shape, F32),
        input_output_aliases={2: 0},
        compiler_params=_params("parallel"),
        name="proj_residual",
    )(a, w, xs, mod)


def _rope_tables(seq, tm, hd, scale):
    q = hd // 4
    pos = jnp.arange(seq)
    row = (pos // GRID_W).astype(F32)[:, None]
    col = (pos % GRID_W).astype(F32)[:, None]
    axis_dim = hd // 2
    inv_freq = ROPE_BASE ** (-jnp.arange(0, axis_dim, 2, dtype=F32) / axis_dim)
    a_row, a_col = row * inv_freq, col * inv_freq
    cos = jnp.concatenate([jnp.cos(a_row), jnp.cos(a_col)] * 2, axis=1)
    sin = jnp.concatenate([-jnp.sin(a_row), -jnp.sin(a_col), jnp.sin(a_row), jnp.sin(a_col)], axis=1)
    assert cos.shape == (seq, 4 * q)
    cos = jnp.concatenate([cos.reshape(seq // tm, tm, hd), jnp.ones((1, tm, hd), F32)], axis=0)
    sin = jnp.concatenate([sin.reshape(seq // tm, tm, hd), jnp.zeros((1, tm, hd), F32)], axis=0)
    return cos * scale, sin * scale, cos, sin


def _head_perm(n_heads, hd):
    q = hd // 4
    one = jnp.concatenate([jnp.arange(0, q), jnp.arange(2 * q, 3 * q), jnp.arange(q, 2 * q), jnp.arange(3 * q, 4 * q)])
    return (jnp.arange(n_heads)[:, None] * hd + one[None, :]).reshape(-1)


def _attn_bias(grp, ctx_len):
    blk = ATT_BLOCK
    qi = jnp.arange(grp * blk)[:, None] % blk
    kj = jnp.arange(3 * blk + ctx_len)[None, :]
    is_ctx = kj >= 3 * blk
    win = jnp.abs(kj - blk - qi) <= ATT_BLOCK
    variants = [win & (kj >= blk), win, win & (kj < 2 * blk), jnp.zeros_like(win)]
    return jnp.stack([jnp.where(is_ctx | m, 0.0, -jnp.inf).astype(F32) for m in variants])


def _gmlp_gate_kernel(x_ref, mod_ref, g_ref, w_ref, gv_ref, ws_ref, bs_ref, z_ref,
                      h_ref, v_ref, mu_ref, rs_ref, *, n_vt, chunk, cg):
    j = pl.program_id(1)
    tm, tn = z_ref.shape

    @pl.when(j == 0)
    def _():
        _normmod_into(h_ref, x_ref, g_ref, mod_ref, 0)

    def gelu_proj(cols):
        y = jnp.dot(h_ref[...], w_ref[:, cols], preferred_element_type=F32)
        return jax.nn.gelu(y)

    @pl.when(j < n_vt)
    def _v_tile():
        for n0 in range(0, tn, GMLP_COL_CHUNK):
            cols = slice(n0, n0 + GMLP_COL_CHUNK)
            v_ref[j, :, cols] = gelu_proj(cols).astype(BF16)

    @pl.when(j == n_vt)
    def _v_stats():
        width = n_vt * tn

        def body(rows):
            tot = jnp.zeros((ROW_CHUNK, 1), F32)
            for k in range(n_vt):
                tot += jnp.sum(v_ref[k, rows, :].astype(F32), axis=-1, keepdims=True)
            mu = tot / width
            sq = jnp.zeros((ROW_CHUNK, 1), F32)
            for k in range(n_vt):
                dv = v_ref[k, rows, :].astype(F32) - mu
                sq += jnp.sum(dv * dv, axis=-1, keepdims=True)
            mu_ref[rows, :] = jnp.broadcast_to(mu, (ROW_CHUNK, LANE))
            rs_ref[rows, :] = jnp.broadcast_to(lax.rsqrt(sq / width + EPS), (ROW_CHUNK, LANE))
        _for_row_chunks(tm, body)

    @pl.when(j >= n_vt)
    def _u_tile():
        jj = j - n_vt
        rep = cg // LANE
        for n0 in range(0, tn, GMLP_COL_CHUNK):
            u = gelu_proj(slice(n0, n0 + GMLP_COL_CHUNK))
            for g0 in range(0, GMLP_COL_CHUNK, cg):
                gi = jj * (tn // cg) + (n0 + g0) // cg
                cols = slice(n0 + g0, n0 + g0 + cg)
                w_s = ws_ref[gi]
                b_s = jnp.concatenate([bs_ref[gi]] * rep, axis=1)
                gv = gv_ref[jj, :, cols]
                for r0 in range(0, tm, chunk):
                    rows = slice(r0, r0 + chunk)
                    mu = jnp.concatenate([mu_ref[rows, :]] * rep, axis=1)
                    rs = jnp.concatenate([rs_ref[rows, :]] * rep, axis=1)
                    vn = ((v_ref[jj, rows, cols].astype(F32) - mu) * rs * gv).astype(BF16)
                    sv = jnp.dot(w_s, vn, preferred_element_type=F32) + b_s
                    z_ref[rows, cols] = (u[rows, g0:g0 + cg] * sv).astype(BF16)


def _gmlp_gate(xs, mod, g, w_in, g_v, w_s, b_s, *, layer, tm, mod_index):
    rows, d = xs.shape
    _, n_tiles, _, tn = w_in.shape
    n_vt = n_tiles // 2
    width = n_vt * tn
    groups, chunk = w_s.shape[1], w_s.shape[2]
    cg = width // groups
    assert tn % GMLP_COL_CHUNK == 0 and GMLP_COL_CHUNK % cg == 0 and cg % LANE == 0 and tm % chunk == 0
    g_v_tiles = g_v[layer].reshape(n_vt, 1, tn)
    b_s_wide = jnp.broadcast_to(b_s[layer][:, :, None], (groups, chunk, LANE))
    return pl.pallas_call(
        functools.partial(_gmlp_gate_kernel, n_vt=n_vt, chunk=chunk, cg=cg),
        grid=(rows // tm, 2 * n_vt),
        in_specs=[
            pl.BlockSpec((tm, d), lambda t, j: (t, 0)),
            pl.BlockSpec((None, N_MOD, d), lambda t, j: (mod_index(t), 0, 0)),
            pl.BlockSpec((1, d), lambda t, j: (0, 0)),
            pl.BlockSpec((None, None, d, tn), lambda t, j: (layer, (j + n_vt) % (2 * n_vt), 0, 0)),
            pl.BlockSpec((n_vt, 1, tn), lambda t, j: (0, 0, 0)),
            pl.BlockSpec((None, groups, chunk, chunk), lambda t, j: (layer, 0, 0, 0)),
            pl.BlockSpec((groups, chunk, LANE), lambda t, j: (0, 0, 0)),
        ],
        out_specs=pl.BlockSpec((tm, tn), lambda t, j: (t, jnp.maximum(j - n_vt, 0))),
        out_shape=jax.ShapeDtypeStruct((rows, width), BF16),
        scratch_shapes=[pltpu.VMEM((tm, d), BF16), pltpu.VMEM((n_vt, tm, tn), BF16),
                        pltpu.VMEM((tm, LANE), F32), pltpu.VMEM((tm, LANE), F32)],
        compiler_params=_params("parallel", "arbitrary"),
        name="gmlp_gate",
    )(xs, mod, g, w_in, g_v_tiles, w_s, b_s_wide)


def _pool_kernel(x_ref, xp_ref, xn_ref, mod_ref, g_ref, w_ref, sc_ref, o_ref, e_ref,
                 *, tm, lat_tiles, seq, ctx_len):
    t = pl.program_id(0)
    is_lat = t < lat_tiles
    seq_len = jnp.where(is_lat, seq, ctx_len)
    tiles_per_seq = jnp.where(is_lat, seq // tm, ctx_len // tm)
    p = jnp.where(is_lat, t % (seq // tm), (t - lat_tiles) % (ctx_len // tm))
    has_prev = (p > 0).astype(F32)
    has_next = (p < tiles_per_seq - 1).astype(F32)

    g, sh, sc = g_ref[...], mod_ref[0:1, :], mod_ref[1:2, :]
    x = x_ref[...]
    halo = POOL_HALO
    e_ref[0:halo, :] = _normmod(xp_ref[...], g, sh, sc) * has_prev
    e_ref[halo:halo + tm, :] = _normmod(x, g, sh, sc)
    e_ref[halo + tm:2 * halo + tm, :] = _normmod(xn_ref[...], g, sh, sc) * has_next

    pos = p * tm + lax.broadcasted_iota(jnp.int32, (tm, 1), 0)
    n_ext = tm + 2 * halo
    cgw = x.shape[1] // len(POOL_SIZES)
    for gi, w in enumerate(POOL_SIZES):
        cols = slice(gi * cgw, (gi + 1) * cgw)
        e = e_ref[:, cols]
        a = e + pltpu.roll(e, 1, axis=0)
        step = 1
        while 2 * step < w:
            a = pltpu.roll(a, step, axis=0) + pltpu.roll(a, n_ext - step, axis=0)
            step *= 2
        cnt = (jnp.minimum(pos + w // 2, seq_len) - jnp.maximum(pos - w // 2, 0)).astype(F32)
        dlt = a[halo:halo + tm, :] / cnt - e[halo:halo + tm, :]
        y = jnp.dot(dlt.astype(BF16), w_ref[gi], preferred_element_type=F32) * sc_ref[:, cols]
        o_ref[:, cols] = x[:, cols] + mod_ref[2:3, cols] * y


def _pool_residual(xs, mod, g, w_pool, scale, *, tm, lat_rows, seq, ctx_len, mod_index):
    rows, d = xs.shape
    hb = tm // POOL_HALO
    n_hblocks = rows // POOL_HALO
    return pl.pallas_call(
        functools.partial(_pool_kernel, tm=tm, lat_tiles=lat_rows // tm, seq=seq, ctx_len=ctx_len),
        grid=(rows // tm,),
        in_specs=[
            pl.BlockSpec((tm, d), lambda t: (t, 0)),
            pl.BlockSpec((POOL_HALO, d), lambda t: (jnp.maximum(t * hb - 1, 0), 0)),
            pl.BlockSpec((POOL_HALO, d), lambda t: (jnp.minimum((t + 1) * hb, n_hblocks - 1), 0)),
            pl.BlockSpec((None, N_MOD, d), lambda t: (mod_index(t), 0, 0)),
            pl.BlockSpec((1, d), lambda t: (0, 0)),
            pl.BlockSpec(w_pool.shape, lambda t: (0, 0, 0)),
            pl.BlockSpec((1, d), lambda t: (0, 0)),
        ],
        out_specs=pl.BlockSpec((tm, d), lambda t: (t, 0)),
        out_shape=jax.ShapeDtypeStruct(xs.shape, F32),
        scratch_shapes=[pltpu.VMEM((tm + 2 * POOL_HALO, d), F32)],
        compiler_params=_params("parallel"),
        name="pool_mixer",
    )(xs, xs, xs, mod, g, w_pool, scale)


def kernel(x, c, ctx, c_ctx, w_mod, b_mod, g_norm, w_up, w_down, attn_wq, attn_wk, attn_wv, attn_wo, attn_sink, gmlp_w_in, gmlp_g_v, gmlp_w_s, gmlp_b_s, gmlp_w_out, pool_w, pool_scale, g_final):
    B, S, D = x.shape
    L = ctx.shape[1]
    depth = w_mod.shape[0]
    n_q = attn_sink.shape[1]
    hd = attn_wq.shape[2] // n_q
    n_kv = attn_wk.shape[2] // hd
    lat_rows, ctx_rows = B * S, B * L

    tm = min(1024, S, ctx_rows)
    tm_qkv = min(512, S, ctx_rows)
    tm_proj = min(512, S, ctx_rows)
    tm_pool = min(256, S, L)
    assert S % tm == 0 and ctx_rows % tm == 0 and S % ATT_BLOCK == 0 and L % ATT_BLOCK == 0
    assert lat_rows % L == 0 and S // ATT_BLOCK >= 2 and S % GRID_W == 0

    def mod_index_for(tile):
        return lambda t: jnp.minimum(t // (S // tile), B)

    assert N_MIXERS == 3 and depth >= 1
    xs = None

    n_cond = -(-(B + 1) // SUBLANE) * SUBLANE
    cond = jnp.concatenate([c, c_ctx[None, :], jnp.zeros((n_cond - B - 1, D), F32)], axis=0)
    mods = _modulation(cond, w_mod, b_mod).reshape(depth, n_cond, N_MOD, D)

    tables = _rope_tables(S, tm_qkv, hd, hd ** -0.5 * LOG2_E)
    bias = _attn_bias(n_q // n_kv, L)
    qperm, kperm = _head_perm(n_q, hd), _head_perm(n_kv, hd)

    def table_index(t):
        return jnp.where(t < lat_rows // tm_qkv, t % (S // tm_qkv), S // tm_qkv)

    def col_tiles(w, tn):
        n_layers, kdim, n = w.shape
        return w.reshape(n_layers, kdim, n // tn, tn).transpose(0, 2, 1, 3)

    w_up_bf, w_down_bf = col_tiles(w_up.astype(BF16), MLP_FF_TILE), w_down.astype(BF16)
    attn_wo_bf = attn_wo.astype(BF16)
    gmlp_w_in_bf = col_tiles(gmlp_w_in.astype(BF16), GMLP_IN_TILE)
    gmlp_w_s_bf, gmlp_w_out_bf = gmlp_w_s.astype(BF16), gmlp_w_out.astype(BF16)

    out = None
    for i in range(depth):
        kind, j = i % N_MIXERS, i // N_MIXERS
        last = i == depth - 1
        mod = mods[i]
        g_a, g_m = g_norm[i, 0][None, :], g_norm[i, 1][None, :]
        res_rows = lat_rows if last else lat_rows + ctx_rows

        if kind == 0:
            wqkv = jnp.concatenate([attn_wq[j][:, qperm], attn_wk[j][:, kperm], attn_wv[j]], axis=1).astype(BF16)
            streams = (x.reshape(lat_rows, D), ctx.reshape(ctx_rows, D)) if i == 0 else (xs,)
            qkv_out = _qkv(streams, mod, g_a, wqkv, tables, tm=tm_qkv, n_q=n_q, n_kv=n_kv, hd=hd,
                           mod_index=mod_index_for(tm_qkv), table_index=table_index)
            q, k, v = qkv_out[:3]
            if i == 0:
                xs = qkv_out[3]
            o = _attention(q, k, v, attn_sink[j] * LOG2_E, bias, batch=B, seq=S, ctx_len=L,
                           n_q=n_q, n_kv=n_kv, hd=hd, ctx_out=not last)
            xs = _proj_residual(o, attn_wo_bf, xs, mod, layer=j, rows=res_rows, tm=tm_proj,
                                mod_index=mod_index_for(tm_proj))
        elif kind == 1:
            z = _gmlp_gate(xs, mod, g_a, gmlp_w_in_bf, gmlp_g_v, gmlp_w_s_bf, gmlp_b_s,
                           layer=j, tm=tm, mod_index=mod_index_for(tm))
            xs = _proj_residual(z, gmlp_w_out_bf, xs, mod, layer=j, rows=res_rows, tm=tm_proj,
                                mod_index=mod_index_for(tm_proj))
        else:
            xs = _pool_residual(xs, mod, g_a, pool_w[j].astype(BF16), pool_scale[j][None, :],
                                tm=tm_pool, lat_rows=lat_rows, seq=S, ctx_len=L,
                                mod_index=mod_index_for(tm_pool))

        res = _mlp(xs, mod, g_m, w_up_bf, w_down_bf, g_final[None, :], layer=i, rows=res_rows, tm=tm, mod_index=mod_index_for(tm), final=last)
        if last:
            out = res
        else:
            xs = res
    return out.reshape(B, S, D)
```

```python
import functools

import jax
import jax.numpy as jnp
from jax import lax
from jax.experimental import pallas as pl
from jax.experimental.pallas import tpu as pltpu

EPS = 1e-6
N_MOD = 6
N_MIXERS = 3
GRID_W = 64
ATT_BLOCK = 128
ROPE_BASE = 10000.0
LOG2_E = 1.4426950408889634
POOL_SIZES = (2, 4, 8, 16)
POOL_HALO = 8
LANE = 128
SUBLANE = 8
VMEM_LIMIT_BYTES = 56 * 1024 * 1024

BF16 = jnp.bfloat16
F32 = jnp.float32


def _params(*sem):
    return pltpu.CompilerParams(dimension_semantics=sem, vmem_limit_bytes=VMEM_LIMIT_BYTES)


def _normmod(x, g, shift, scale):
    ms = jnp.mean(x * x, axis=-1, keepdims=True)
    return (x * lax.rsqrt(ms + EPS)) * g * (1.0 + scale) + shift


ROW_CHUNK = 128
MLP_OUT_CHUNK = 512
MLP_ROW_CHUNK = 256
GMLP_COL_CHUNK = 256
MLP_FF_TILE = 1024
GMLP_IN_TILE = 1024
GMLP_ROW_BLOCK = 512


def _for_row_chunks(n_rows, body):
    def step(i, carry):
        body(pl.ds(pl.multiple_of(i * ROW_CHUNK, ROW_CHUNK), ROW_CHUNK))
        return carry
    lax.fori_loop(0, n_rows // ROW_CHUNK, step, 0)


def _normmod_into(dst_ref, x_ref, g_ref, mod_ref, shift_row):
    def body(rows):
        h = _normmod(x_ref[rows, :], g_ref[...], mod_ref[shift_row:shift_row + 1, :],
                     mod_ref[shift_row + 1:shift_row + 2, :])
        dst_ref[rows, :] = h.astype(dst_ref.dtype)
    _for_row_chunks(x_ref.shape[0], body)


def _mod_kernel(c_ref, w_ref, b_ref, o_ref):
    c = c_ref[...]
    s = (c * jax.nn.sigmoid(c)).astype(BF16)
    o_ref[...] = jnp.dot(s, w_ref[...].astype(BF16), preferred_element_type=F32) + b_ref[...]


def _modulation(cond, w_mod, b_mod):
    depth, d, n = w_mod.shape
    r = cond.shape[0]
    tn = 1024
    return pl.pallas_call(
        _mod_kernel,
        grid=(depth, n // tn),
        in_specs=[
            pl.BlockSpec((r, d), lambda i, j: (0, 0)),
            pl.BlockSpec((None, d, tn), lambda i, j: (i, 0, j)),
            pl.BlockSpec((None, 1, tn), lambda i, j: (i, 0, j)),
        ],
        out_specs=pl.BlockSpec((None, r, tn), lambda i, j: (i, 0, j)),
        out_shape=jax.ShapeDtypeStruct((depth, r, n), F32),
        compiler_params=_params("parallel", "parallel"),
        name="modulation",
    )(cond, w_mod, b_mod.reshape(depth, 1, n))


def _mlp_kernel(x_hbm, mod_ref, g_ref, wu_ref, wd_ref, gf_ref, o_ref, x_buf, h_ref, x_sem, *, final):
    t, f = pl.program_id(0), pl.program_id(1)
    nt, nf = pl.num_programs(0), pl.num_programs(1)
    tm, d = o_ref.shape

    def x_copy(tile):
        return pltpu.make_async_copy(x_hbm.at[pl.ds(tile * tm, tm), :], x_buf, x_sem)

    def up(h):
        u = jnp.dot(h, wu_ref[...], preferred_element_type=F32)
        return jnp.square(jnp.maximum(u, 0.0)).astype(BF16)

    def gate(cols):
        return mod_ref[5:6, cols]

    @pl.when(jnp.logical_and(t == 0, f == 0))
    def _fetch_first_tile():
        x_copy(0).start()

    @pl.when(f == 0)
    def _first():
        x_copy(t).wait()
        for r0 in range(0, tm, MLP_ROW_CHUNK):
            rows = slice(r0, r0 + MLP_ROW_CHUNK)
            h = _normmod(x_buf[rows, :], g_ref[...], mod_ref[3:4, :], mod_ref[4:5, :]).astype(BF16)
            h_ref[rows, :] = h
            u = up(h)
            for n0 in range(0, d, MLP_OUT_CHUNK):
                cols = slice(n0, n0 + MLP_OUT_CHUNK)
                y = jnp.dot(u, wd_ref[:, cols], preferred_element_type=F32)
                o_ref[rows, cols] = x_buf[rows, cols] + gate(cols) * y

    @pl.when(jnp.logical_and(f == 1, t + 1 < nt))
    def _prefetch_next_tile():
        x_copy(t + 1).start()

    @pl.when(jnp.logical_and(f > 0, f < nf - 1))
    def _middle():
        u = up(h_ref[...])
        for n0 in range(0, d, MLP_OUT_CHUNK):
            cols = slice(n0, n0 + MLP_OUT_CHUNK)
            o_ref[:, cols] += gate(cols) * jnp.dot(u, wd_ref[:, cols], preferred_element_type=F32)

    @pl.when(f == nf - 1)
    def _last():
        for r0 in range(0, tm, MLP_ROW_CHUNK):
            rows = slice(r0, r0 + MLP_ROW_CHUNK)
            u = up(h_ref[rows, :])
            y = o_ref[rows, :] + mod_ref[5:6, :] * jnp.dot(u, wd_ref[...], preferred_element_type=F32)
            if final:
                ms = jnp.mean(y * y, axis=-1, keepdims=True)
                y = (y * lax.rsqrt(ms + EPS)) * gf_ref[...]
            o_ref[rows, :] = y


def _mlp(xs, mod, g, w_up, w_down, g_final, *, layer, rows, tm, mod_index, final):
    _, d, ff = w_up.shape
    tf = MLP_FF_TILE
    assert ff // tf >= 3 and tm % MLP_ROW_CHUNK == 0
    return pl.pallas_call(
        functools.partial(_mlp_kernel, final=final),
        grid=(rows // tm, ff // tf),
        in_specs=[
            pl.BlockSpec(memory_space=pl.ANY),
            pl.BlockSpec((None, N_MOD, d), lambda t, f: (mod_index(t), 0, 0)),
            pl.BlockSpec((1, d), lambda t, f: (0, 0)),
            pl.BlockSpec((None, d, tf), lambda t, f: (layer, 0, f)),
            pl.BlockSpec((None, tf, d), lambda t, f: (layer, f, 0)),
            pl.BlockSpec((1, d), lambda t, f: (0, 0)),
        ],
        out_specs=pl.BlockSpec((tm, d), lambda t, f: (t, 0)),
        out_shape=jax.ShapeDtypeStruct((rows, d), F32),
        scratch_shapes=[pltpu.VMEM((tm, d), F32), pltpu.VMEM((tm, d), BF16), pltpu.SemaphoreType.DMA(())],
        compiler_params=_params("arbitrary", "arbitrary"),
        name="mlp_final" if final else "mlp",
    )(xs, mod, g, w_up, w_down, g_final)


def _qkv_kernel(*refs, n_q, n_kv, hd, lat_tiles):
    if lat_tiles is None:
        x_ref, mod_ref, g_ref, w_ref, cq_ref, sq_ref, ck_ref, sk_ref, q_ref, k_ref, v_ref = refs
    else:
        (xl_ref, xc_ref, mod_ref, g_ref, w_ref, cq_ref, sq_ref, ck_ref, sk_ref,
         q_ref, k_ref, v_ref, xs_ref) = refs
        is_lat = pl.program_id(0) < lat_tiles
    group = 4
    for r0 in range(0, q_ref.shape[0], MLP_ROW_CHUNK):
        rows = slice(r0, r0 + MLP_ROW_CHUNK)
        if lat_tiles is None:
            x = x_ref[rows, :]
        else:
            x = jnp.where(is_lat, xl_ref[rows, :], xc_ref[rows, :])
            xs_ref[rows, :] = x
        h = _normmod(x, g_ref[...], mod_ref[0:1, :], mod_ref[1:2, :]).astype(BF16)

        def rope(y, c_ref, s_ref):
            return y * c_ref[rows, :] + pltpu.roll(y, hd // 2, axis=1) * s_ref[rows, :]

        for h0 in range(0, n_q, group):
            acc = jnp.dot(h, w_ref[:, h0 * hd:(h0 + group) * hd], preferred_element_type=F32)
            for j in range(group):
                y = acc[:, j * hd:(j + 1) * hd]
                q_ref[rows, (h0 + j) * hd:(h0 + j + 1) * hd] = rope(y, cq_ref, sq_ref).astype(BF16)
        acc = jnp.dot(h, w_ref[:, n_q * hd:(n_q + n_kv) * hd], preferred_element_type=F32)
        for j in range(n_kv):
            k_ref[rows, j * hd:(j + 1) * hd] = rope(acc[:, j * hd:(j + 1) * hd], ck_ref, sk_ref).astype(BF16)
        acc = jnp.dot(h, w_ref[:, (n_q + n_kv) * hd:(n_q + 2 * n_kv) * hd], preferred_element_type=F32)
        v_ref[rows, :] = acc.astype(BF16)


def _qkv(streams, mod, g, wqkv, tables, *, tm, n_q, n_kv, hd, mod_index, table_index):
    split = len(streams) == 2
    rows = sum(s.shape[0] for s in streams)
    d = streams[0].shape[1]
    lat_tiles = streams[0].shape[0] // tm if split else None
    tab_spec = pl.BlockSpec((None, tm, hd), lambda t: (table_index(t), 0, 0))
    if split:
        x_specs = [pl.BlockSpec((tm, d), lambda t: (jnp.minimum(t, lat_tiles - 1), 0)),
                   pl.BlockSpec((tm, d), lambda t: (jnp.maximum(t - lat_tiles, 0), 0))]
    else:
        x_specs = [pl.BlockSpec((tm, d), lambda t: (t, 0))]
    out_specs = [
        pl.BlockSpec((tm, n_q * hd), lambda t: (t, 0)),
        pl.BlockSpec((tm, n_kv * hd), lambda t: (t, 0)),
        pl.BlockSpec((tm, n_kv * hd), lambda t: (t, 0)),
    ]
    out_shape = [
        jax.ShapeDtypeStruct((rows, n_q * hd), BF16),
        jax.ShapeDtypeStruct((rows, n_kv * hd), BF16),
        jax.ShapeDtypeStruct((rows, n_kv * hd), BF16),
    ]
    if split:
        out_specs.append(pl.BlockSpec((tm, d), lambda t: (t, 0)))
        out_shape.append(jax.ShapeDtypeStruct((rows, d), F32))
    return pl.pallas_call(
        functools.partial(_qkv_kernel, n_q=n_q, n_kv=n_kv, hd=hd, lat_tiles=lat_tiles),
        grid=(rows // tm,),
        in_specs=x_specs + [
            pl.BlockSpec((None, N_MOD, d), lambda t: (mod_index(t), 0, 0)),
            pl.BlockSpec((1, d), lambda t: (0, 0)),
            pl.BlockSpec(wqkv.shape, lambda t: (0, 0), pipeline_mode=pl.Buffered(1)),
            tab_spec, tab_spec, tab_spec, tab_spec,
        ],
        out_specs=out_specs,
        out_shape=out_shape,
        compiler_params=_params("parallel"),
        name="qkv_rope_join" if split else "qkv_rope",
    )(*streams, mod, g, wqkv, *tables)


def _attn_kernel(sink_ref, q_ref, kp_ref, kc_ref, kn_ref, kx_ref, vp_ref, vc_ref, vn_ref, vx_ref,
                 bias_ref, o_ref, *, n_kv, grp, hd):
    bias = bias_ref[...]
    blk = q_ref.shape[0]
    n_keys = bias.shape[1]
    ones = jnp.ones((n_keys, hd), BF16)
    for hk in range(n_kv):
        sl = slice(hk * hd, (hk + 1) * hd)
        k_all = jnp.concatenate([kp_ref[:, sl], kc_ref[:, sl], kn_ref[:, sl], kx_ref[:, sl]], axis=0)
        v_all = jnp.concatenate([vp_ref[:, sl], vc_ref[:, sl], vn_ref[:, sl], vx_ref[:, sl]], axis=0)
        v_aug = jnp.concatenate([v_all, ones], axis=1)
        heads = [hk * grp + j for j in range(grp)]
        qh = jnp.concatenate([q_ref[:, hq * hd:(hq + 1) * hd] for hq in heads], axis=0)
        s = lax.dot_general(qh, k_all, (((1,), (1,)), ((), ())), preferred_element_type=F32) + bias
        sink = jnp.concatenate([jnp.full((blk, 1), sink_ref[hq], F32) for hq in heads], axis=0)
        m = jnp.maximum(jnp.max(s, axis=-1, keepdims=True), sink)
        p = jnp.exp2(s - m).astype(BF16)
        ov = jnp.dot(p, v_aug, preferred_element_type=F32)
        o = ov[:, :hd] / (ov[:, hd:] + jnp.exp2(sink - m))
        for j, hq in enumerate(heads):
            o_ref[:, hq * hd:(hq + 1) * hd] = o[j * blk:(j + 1) * blk, :].astype(o_ref.dtype)


def _attention(q, k, v, sink, bias, *, batch, seq, ctx_len, n_q, n_kv, hd, ctx_out):
    rows = q.shape[0]
    blk = ATT_BLOCK
    nb = seq // blk
    ncb = ctx_len // blk
    lat_blocks = batch * nb
    nq_blocks = nb + (ncb if ctx_out else 0)

    def q_idx(b, n):
        return (jnp.where(n < nb, b * nb + n, lat_blocks + b * ncb + (n - nb)), 0)

    def kv_idx(off):
        return lambda b, n: (b * nb + jnp.clip(n + off, 0, nb - 1), 0)

    def ctx_idx(b, n):
        return (lat_blocks * blk // ctx_len + b, 0)

    def bias_idx(b, n):
        return (jnp.where(n >= nb, 3, jnp.where(n == 0, 0, jnp.where(n == nb - 1, 2, 1))), 0, 0)

    kv_w = n_kv * hd
    kv_specs = [pl.BlockSpec((blk, kv_w), kv_idx(-1)), pl.BlockSpec((blk, kv_w), kv_idx(0)),
                pl.BlockSpec((blk, kv_w), kv_idx(1)), pl.BlockSpec((ctx_len, kv_w), ctx_idx)]
    return pl.pallas_call(
        functools.partial(_attn_kernel, n_kv=n_kv, grp=n_q // n_kv, hd=hd),
        grid=(batch, nq_blocks),
        in_specs=[pl.BlockSpec(memory_space=pltpu.SMEM),
                  pl.BlockSpec((blk, n_q * hd), q_idx)] + kv_specs + kv_specs
                 + [pl.BlockSpec((None,) + bias.shape[1:], bias_idx)],
        out_specs=pl.BlockSpec((blk, n_q * hd), q_idx),
        out_shape=jax.ShapeDtypeStruct((rows if ctx_out else lat_blocks * blk, n_q * hd), BF16),
        compiler_params=_params("parallel", "parallel"),
        name="window_attention",
    )(sink, q, k, k, k, k, v, v, v, v, bias)


def _proj_res_kernel(a_ref, w_ref, x_ref, mod_ref, o_ref):
    a = a_ref[...]
    for n0 in range(0, o_ref.shape[1], MLP_OUT_CHUNK):
        cols = slice(n0, n0 + MLP_OUT_CHUNK)
        y = jnp.dot(a, w_ref[:, cols], preferred_element_type=F32)
        o_ref[:, cols] = x_ref[:, cols] + mod_ref[2:3, cols] * y


def _proj_residual(a, w, xs, mod, *, layer, rows, tm, mod_index):
    _, kdim, d = w.shape
    return pl.pallas_call(
        _proj_res_kernel,
        grid=(rows // tm,),
        in_specs=[
            pl.BlockSpec((tm, kdim), lambda t: (t, 0)),
            pl.BlockSpec((None, kdim, d), lambda t: (layer, 0, 0), pipeline_mode=pl.Buffered(1)),
            pl.BlockSpec((tm, d), lambda t: (t, 0)),
            pl.BlockSpec((None, N_MOD, d), lambda t: (mod_index(t), 0, 0)),
        ],
        out_specs=pl.BlockSpec((tm, d), lambda t: (t, 0)),
        out_shape=jax.ShapeDtypeStruct(xs.shape, F32),
        input_output_aliases={2: 0},
        compiler_params=_params("parallel"),
        name="proj_residual",
    )(a, w, xs, mod)


def _rope_tables(seq, tm, hd, scale):
    q = hd // 4
    pos = jnp.arange(seq)
    row = (pos // GRID_W).astype(F32)[:, None]
    col = (pos % GRID_W).astype(F32)[:, None]
    axis_dim = hd // 2
    inv_freq = ROPE_BASE ** (-jnp.arange(0, axis_dim, 2, dtype=F32) / axis_dim)
    a_row, a_col = row * inv_freq, col * inv_freq
    cos = jnp.concatenate([jnp.cos(a_row), jnp.cos(a_col)] * 2, axis=1)
    sin = jnp.concatenate([-jnp.sin(a_row), -jnp.sin(a_col), jnp.sin(a_row), jnp.sin(a_col)], axis=1)
    assert cos.shape == (seq, 4 * q)
    cos = jnp.concatenate([cos.reshape(seq // tm, tm, hd), jnp.ones((1, tm, hd), F32)], axis=0)
    sin = jnp.concatenate([sin.reshape(seq // tm, tm, hd), jnp.zeros((1, tm, hd), F32)], axis=0)
    return cos * scale, sin * scale, cos, sin


def _head_perm(n_heads, hd):
    q = hd // 4
    one = jnp.concatenate([jnp.arange(0, q), jnp.arange(2 * q, 3 * q), jnp.arange(q, 2 * q), jnp.arange(3 * q, 4 * q)])
    return (jnp.arange(n_heads)[:, None] * hd + one[None, :]).reshape(-1)


def _attn_bias(grp, ctx_len):
    blk = ATT_BLOCK
    qi = jnp.arange(grp * blk)[:, None] % blk
    kj = jnp.arange(3 * blk + ctx_len)[None, :]
    is_ctx = kj >= 3 * blk
    win = jnp.abs(kj - blk - qi) <= ATT_BLOCK
    variants = [win & (kj >= blk), win, win & (kj < 2 * blk), jnp.zeros_like(win)]
    return jnp.stack([jnp.where(is_ctx | m, 0.0, -jnp.inf).astype(F32) for m in variants])


def _gmlp_gate_kernel(x_ref, mod_ref, g_ref, w_ref, gv_ref, ws_ref, bs_ref, z_ref,
                      h_ref, v_ref, mu_ref, rs_ref, *, n_vt, chunk, cg):
    j = pl.program_id(1)
    tm, tn = z_ref.shape

    def gelu_proj(h, cols):
        return jax.nn.gelu(jnp.dot(h, w_ref[:, cols], preferred_element_type=F32))

    def v_cols(h, rows, tile):
        tot = jnp.zeros((h.shape[0], 1), F32)
        for n0 in range(0, tn, GMLP_COL_CHUNK):
            cols = slice(n0, n0 + GMLP_COL_CHUNK)
            y = gelu_proj(h, cols)
            v_ref[tile, rows, cols] = y.astype(BF16)
            tot += jnp.sum(y, axis=-1, keepdims=True)
        return tot

    @pl.when(j == 0)
    def _first_v_tile():
        for r0 in range(0, tm, MLP_ROW_CHUNK):
            rows = slice(r0, r0 + MLP_ROW_CHUNK)
            h = _normmod(x_ref[rows, :], g_ref[...], mod_ref[0:1, :], mod_ref[1:2, :]).astype(BF16)
            h_ref[rows, :] = h
            mu_ref[rows, :] = jnp.broadcast_to(v_cols(h, rows, 0), (MLP_ROW_CHUNK, LANE))

    row_block = min(GMLP_ROW_BLOCK, tm)

    @pl.when(jnp.logical_and(j > 0, j < n_vt))
    def _v_tile():
        for rb in range(0, tm, row_block):
            rows = slice(rb, rb + row_block)
            mu_ref[rows, :] += jnp.broadcast_to(v_cols(h_ref[rows, :], rows, j), (row_block, LANE))

    @pl.when(j == n_vt)
    def _v_stats():
        width = n_vt * tn

        def body(rows):
            mu = mu_ref[rows, :] / width
            mu_wide = jnp.concatenate([mu] * (tn // LANE), axis=1)
            sq = jnp.zeros((ROW_CHUNK, 1), F32)
            for k in range(n_vt):
                dv = v_ref[k, rows, :].astype(F32) - mu_wide
                sq += jnp.sum(dv * dv, axis=-1, keepdims=True)
            mu_ref[rows, :] = mu
            rs_ref[rows, :] = jnp.broadcast_to(lax.rsqrt(sq / width + EPS), (ROW_CHUNK, LANE))
        _for_row_chunks(tm, body)

    @pl.when(j >= n_vt)
    def _u_tile():
        jj = j - n_vt
        rep = cg // LANE
        for rb in range(0, tm, row_block):
            h = h_ref[rb:rb + row_block, :]
            for n0 in range(0, tn, GMLP_COL_CHUNK):
                u = gelu_proj(h, slice(n0, n0 + GMLP_COL_CHUNK))
                for g0 in range(0, GMLP_COL_CHUNK, cg):
                    gi = jj * (tn // cg) + (n0 + g0) // cg
                    cols = slice(n0 + g0, n0 + g0 + cg)
                    w_s = ws_ref[gi]
                    b_s = jnp.concatenate([bs_ref[gi]] * rep, axis=1)
                    gv = gv_ref[jj, :, cols]
                    for r0 in range(0, row_block, chunk):
                        rows = slice(rb + r0, rb + r0 + chunk)
                        mu = jnp.concatenate([mu_ref[rows, :]] * rep, axis=1)
                        rs = jnp.concatenate([rs_ref[rows, :]] * rep, axis=1)
                        vn = ((v_ref[jj, rows, cols].astype(F32) - mu) * rs * gv).astype(BF16)
                        sv = jnp.dot(w_s, vn, preferred_element_type=F32) + b_s
                        z_ref[rows, cols] = (u[r0:r0 + chunk, g0:g0 + cg] * sv).astype(BF16)


def _gmlp_gate(xs, mod, g, w_in, g_v, w_s, b_s, *, layer, tm, mod_index):
    rows, d = xs.shape
    tn = GMLP_IN_TILE
    width = w_in.shape[2] // 2
    n_vt = width // tn
    groups, chunk = w_s.shape[1], w_s.shape[2]
    cg = width // groups
    assert tn % GMLP_COL_CHUNK == 0 and GMLP_COL_CHUNK % cg == 0 and cg % LANE == 0 and tm % chunk == 0
    g_v_tiles = g_v[layer].reshape(n_vt, 1, tn)
    b_s_wide = jnp.broadcast_to(b_s[layer][:, :, None], (groups, chunk, LANE))
    return pl.pallas_call(
        functools.partial(_gmlp_gate_kernel, n_vt=n_vt, chunk=chunk, cg=cg),
        grid=(rows // tm, 2 * n_vt),
        in_specs=[
            pl.BlockSpec((tm, d), lambda t, j: (t, 0)),
            pl.BlockSpec((None, N_MOD, d), lambda t, j: (mod_index(t), 0, 0)),
            pl.BlockSpec((1, d), lambda t, j: (0, 0)),
            pl.BlockSpec((None, d, tn), lambda t, j: (layer, 0, (j + n_vt) % (2 * n_vt))),
            pl.BlockSpec((n_vt, 1, tn), lambda t, j: (0, 0, 0)),
            pl.BlockSpec((None, groups, chunk, chunk), lambda t, j: (layer, 0, 0, 0)),
            pl.BlockSpec((groups, chunk, LANE), lambda t, j: (0, 0, 0)),
        ],
        out_specs=pl.BlockSpec((tm, tn), lambda t, j: (t, jnp.maximum(j - n_vt, 0))),
        out_shape=jax.ShapeDtypeStruct((rows, width), BF16),
        scratch_shapes=[pltpu.VMEM((tm, d), BF16), pltpu.VMEM((n_vt, tm, tn), BF16),
                        pltpu.VMEM((tm, LANE), F32), pltpu.VMEM((tm, LANE), F32)],
        compiler_params=_params("parallel", "arbitrary"),
        name="gmlp_gate",
    )(xs, mod, g, w_in, g_v_tiles, w_s, b_s_wide)


def _pool_kernel(x_ref, xp_ref, xn_ref, mod_ref, g_ref, w_ref, sc_ref, o_ref, e_ref,
                 *, tm, lat_tiles, seq, ctx_len):
    t = pl.program_id(0)
    is_lat = t < lat_tiles
    seq_len = jnp.where(is_lat, seq, ctx_len)
    tiles_per_seq = jnp.where(is_lat, seq // tm, ctx_len // tm)
    p = jnp.where(is_lat, t % (seq // tm), (t - lat_tiles) % (ctx_len // tm))
    has_prev = (p > 0).astype(F32)
    has_next = (p < tiles_per_seq - 1).astype(F32)

    g, sh, sc = g_ref[...], mod_ref[0:1, :], mod_ref[1:2, :]
    x = x_ref[...]
    halo = POOL_HALO
    e_ref[0:halo, :] = _normmod(xp_ref[...], g, sh, sc) * has_prev
    e_ref[halo:halo + tm, :] = _normmod(x, g, sh, sc)
    e_ref[halo + tm:2 * halo + tm, :] = _normmod(xn_ref[...], g, sh, sc) * has_next

    pos = p * tm + lax.broadcasted_iota(jnp.int32, (tm, 1), 0)
    n_ext = tm + 2 * halo
    cgw = x.shape[1] // len(POOL_SIZES)
    for gi, w in enumerate(POOL_SIZES):
        cols = slice(gi * cgw, (gi + 1) * cgw)
        e = e_ref[:, cols]
        a = e + pltpu.roll(e, 1, axis=0)
        step = 1
        while 2 * step < w:
            a = pltpu.roll(a, step, axis=0) + pltpu.roll(a, n_ext - step, axis=0)
            step *= 2
        cnt = (jnp.minimum(pos + w // 2, seq_len) - jnp.maximum(pos - w // 2, 0)).astype(F32)
        dlt = a[halo:halo + tm, :] / cnt - e[halo:halo + tm, :]
        y = jnp.dot(dlt.astype(BF16), w_ref[gi], preferred_element_type=F32) * sc_ref[:, cols]
        o_ref[:, cols] = x[:, cols] + mod_ref[2:3, cols] * y


def _pool_residual(xs, mod, g, w_pool, scale, *, tm, lat_rows, seq, ctx_len, mod_index):
    rows, d = xs.shape
    hb = tm // POOL_HALO
    n_hblocks = rows // POOL_HALO
    return pl.pallas_call(
        functools.partial(_pool_kernel, tm=tm, lat_tiles=lat_rows // tm, seq=seq, ctx_len=ctx_len),
        grid=(rows // tm,),
        in_specs=[
            pl.BlockSpec((tm, d), lambda t: (t, 0)),
            pl.BlockSpec((POOL_HALO, d), lambda t: (jnp.maximum(t * hb - 1, 0), 0)),
            pl.BlockSpec((POOL_HALO, d), lambda t: (jnp.minimum((t + 1) * hb, n_hblocks - 1), 0)),
            pl.BlockSpec((None, N_MOD, d), lambda t: (mod_index(t), 0, 0)),
            pl.BlockSpec((1, d), lambda t: (0, 0)),
            pl.BlockSpec(w_pool.shape, lambda t: (0, 0, 0)),
            pl.BlockSpec((1, d), lambda t: (0, 0)),
        ],
        out_specs=pl.BlockSpec((tm, d), lambda t: (t, 0)),
        out_shape=jax.ShapeDtypeStruct(xs.shape, F32),
        scratch_shapes=[pltpu.VMEM((tm + 2 * POOL_HALO, d), F32)],
        compiler_params=_params("parallel"),
        name="pool_mixer",
    )(xs, xs, xs, mod, g, w_pool, scale)


def kernel(x, c, ctx, c_ctx, w_mod, b_mod, g_norm, w_up, w_down, attn_wq, attn_wk, attn_wv, attn_wo, attn_sink, gmlp_w_in, gmlp_g_v, gmlp_w_s, gmlp_b_s, gmlp_w_out, pool_w, pool_scale, g_final):
    B, S, D = x.shape
    L = ctx.shape[1]
    depth = w_mod.shape[0]
    n_q = attn_sink.shape[1]
    hd = attn_wq.shape[2] // n_q
    n_kv = attn_wk.shape[2] // hd
    lat_rows, ctx_rows = B * S, B * L

    tm = min(1024, S, ctx_rows)
    tm_qkv = min(512, S, ctx_rows)
    tm_proj = min(512, S, ctx_rows)
    tm_pool = min(256, S, L)
    assert S % tm == 0 and ctx_rows % tm == 0 and S % ATT_BLOCK == 0 and L % ATT_BLOCK == 0
    assert lat_rows % L == 0 and S // ATT_BLOCK >= 2 and S % GRID_W == 0

    def mod_index_for(tile):
        return lambda t: jnp.minimum(t // (S // tile), B)

    assert N_MIXERS == 3 and depth >= 1
    xs = None

    n_cond = -(-(B + 1) // SUBLANE) * SUBLANE
    cond = jnp.concatenate([c, c_ctx[None, :], jnp.zeros((n_cond - B - 1, D), F32)], axis=0)
    mods = _modulation(cond, w_mod, b_mod).reshape(depth, n_cond, N_MOD, D)

    tables = _rope_tables(S, tm_qkv, hd, hd ** -0.5 * LOG2_E)
    bias = _attn_bias(n_q // n_kv, L)
    qperm, kperm = _head_perm(n_q, hd), _head_perm(n_kv, hd)

    def table_index(t):
        return jnp.where(t < lat_rows // tm_qkv, t % (S // tm_qkv), S // tm_qkv)

    w_up_bf, w_down_bf = w_up.astype(BF16), w_down.astype(BF16)
    attn_wo_bf = attn_wo.astype(BF16)
    gmlp_w_in_bf, gmlp_w_s_bf, gmlp_w_out_bf = gmlp_w_in.astype(BF16), gmlp_w_s.astype(BF16), gmlp_w_out.astype(BF16)

    out = None
    for i in range(depth):
        kind, j = i % N_MIXERS, i // N_MIXERS
        last = i == depth - 1
        mod = mods[i]
        g_a, g_m = g_norm[i, 0][None, :], g_norm[i, 1][None, :]
        res_rows = lat_rows if last else lat_rows + ctx_rows

        if kind == 0:
            wqkv = jnp.concatenate([attn_wq[j][:, qperm], attn_wk[j][:, kperm], attn_wv[j]], axis=1).astype(BF16)
            streams = (x.reshape(lat_rows, D), ctx.reshape(ctx_rows, D)) if i == 0 else (xs,)
            qkv_out = _qkv(streams, mod, g_a, wqkv, tables, tm=tm_qkv, n_q=n_q, n_kv=n_kv, hd=hd,
                           mod_index=mod_index_for(tm_qkv), table_index=table_index)
            q, k, v = qkv_out[:3]
            if i == 0:
                xs = qkv_out[3]
            o = _attention(q, k, v, attn_sink[j] * LOG2_E, bias, batch=B, seq=S, ctx_len=L,
                           n_q=n_q, n_kv=n_kv, hd=hd, ctx_out=not last)
            xs = _proj_residual(o, attn_wo_bf, xs, mod, layer=j, rows=res_rows, tm=tm_proj,
                                mod_index=mod_index_for(tm_proj))
        elif kind == 1:
            z = _gmlp_gate(xs, mod, g_a, gmlp_w_in_bf, gmlp_g_v, gmlp_w_s_bf, gmlp_b_s,
                           layer=j, tm=tm, mod_index=mod_index_for(tm))
            xs = _proj_residual(z, gmlp_w_out_bf, xs, mod, layer=j, rows=res_rows, tm=tm_proj,
                                mod_index=mod_index_for(tm_proj))
        else:
            xs = _pool_residual(xs, mod, g_a, pool_w[j].astype(BF16), pool_scale[j][None, :],
                                tm=tm_pool, lat_rows=lat_rows, seq=S, ctx_len=L,
                                mod_index=mod_index_for(tm_pool))

        res = _mlp(xs, mod, g_m, w_up_bf, w_down_bf, g_final[None, :], layer=i, rows=res_rows, tm=tm, mod_index=mod_index_for(tm), final=last)
        if last:
            out = res
        else:
            xs = res
    return out.reshape(B, S, D)
```

```python
import functools

import jax
import jax.numpy as jnp
from jax import lax
from jax.experimental import pallas as pl
from jax.experimental.pallas import tpu as pltpu

EPS = 1e-6
N_MOD = 6
N_MIXERS = 3
GRID_W = 64
ATT_BLOCK = 128
ROPE_BASE = 10000.0
LOG2_E = 1.4426950408889634
POOL_SIZES = (2, 4, 8, 16)
POOL_HALO = 8
LANE = 128
SUBLANE = 8
VMEM_LIMIT_BYTES = 56 * 1024 * 1024

BF16 = jnp.bfloat16
F32 = jnp.float32


def _params(*sem):
    return pltpu.CompilerParams(dimension_semantics=sem, vmem_limit_bytes=VMEM_LIMIT_BYTES)


def _normmod(x, g, shift, scale):
    ms = jnp.mean(x * x, axis=-1, keepdims=True)
    return (x * lax.rsqrt(ms + EPS)) * g * (1.0 + scale) + shift


ROW_CHUNK = 128
MLP_OUT_CHUNK = 512
MLP_ROW_CHUNK = 256
GMLP_COL_CHUNK = 256
MLP_FF_TILE = 1024
GMLP_IN_TILE = 1024
GMLP_ROW_BLOCK = 512
MLP_CAST_STEPS = 128


def _for_row_chunks(n_rows, body):
    def step(i, carry):
        body(pl.ds(pl.multiple_of(i * ROW_CHUNK, ROW_CHUNK), ROW_CHUNK))
        return carry
    lax.fori_loop(0, n_rows // ROW_CHUNK, step, 0)


def _normmod_into(dst_ref, x_ref, g_ref, mod_ref, shift_row):
    def body(rows):
        h = _normmod(x_ref[rows, :], g_ref[...], mod_ref[shift_row:shift_row + 1, :],
                     mod_ref[shift_row + 1:shift_row + 2, :])
        dst_ref[rows, :] = h.astype(dst_ref.dtype)
    _for_row_chunks(x_ref.shape[0], body)


def _mod_kernel(c_ref, w_ref, b_ref, o_ref):
    c = c_ref[...]
    s = (c * jax.nn.sigmoid(c)).astype(BF16)
    o_ref[...] = jnp.dot(s, w_ref[...].astype(BF16), preferred_element_type=F32) + b_ref[...]


def _modulation(cond, w_mod, b_mod):
    depth, d, n = w_mod.shape
    r = cond.shape[0]
    tn = 1024
    return pl.pallas_call(
        _mod_kernel,
        grid=(depth, n // tn),
        in_specs=[
            pl.BlockSpec((r, d), lambda i, j: (0, 0)),
            pl.BlockSpec((None, d, tn), lambda i, j: (i, 0, j)),
            pl.BlockSpec((None, 1, tn), lambda i, j: (i, 0, j)),
        ],
        out_specs=pl.BlockSpec((None, r, tn), lambda i, j: (i, 0, j)),
        out_shape=jax.ShapeDtypeStruct((depth, r, n), F32),
        compiler_params=_params("parallel", "parallel"),
        name="modulation",
    )(cond, w_mod, b_mod.reshape(depth, 1, n))


def _mlp_kernel(*refs, final, n_cast):
    if n_cast:
        (x_hbm, mod_ref, g_ref, wu_ref, wd_ref, gf_ref, wun_ref, wdn_ref,
         o_ref, wun_o, wdn_o, x_buf, h_ref, x_sem) = refs
    else:
        x_hbm, mod_ref, g_ref, wu_ref, wd_ref, gf_ref, o_ref, x_buf, h_ref, x_sem = refs
    t, f = pl.program_id(0), pl.program_id(1)
    nt, nf = pl.num_programs(0), pl.num_programs(1)
    tm, d = o_ref.shape

    if n_cast:
        @pl.when(t * nf + f < n_cast)
        def _cast_next_layer_weights():
            wun_o[...] = wun_ref[...].astype(BF16)
            wdn_o[...] = wdn_ref[...].astype(BF16)

    def x_copy(tile):
        return pltpu.make_async_copy(x_hbm.at[pl.ds(tile * tm, tm), :], x_buf, x_sem)

    def up(h):
        u = jnp.dot(h, wu_ref[...], preferred_element_type=F32)
        return jnp.square(jnp.maximum(u, 0.0)).astype(BF16)

    def gate(cols):
        return mod_ref[5:6, cols]

    @pl.when(jnp.logical_and(t == 0, f == 0))
    def _fetch_first_tile():
        x_copy(0).start()

    @pl.when(f == 0)
    def _first():
        x_copy(t).wait()
        for r0 in range(0, tm, MLP_ROW_CHUNK):
            rows = slice(r0, r0 + MLP_ROW_CHUNK)
            h = _normmod(x_buf[rows, :], g_ref[...], mod_ref[3:4, :], mod_ref[4:5, :]).astype(BF16)
            h_ref[rows, :] = h
            u = up(h)
            for n0 in range(0, d, MLP_OUT_CHUNK):
                cols = slice(n0, n0 + MLP_OUT_CHUNK)
                y = jnp.dot(u, wd_ref[:, cols], preferred_element_type=F32)
                o_ref[rows, cols] = x_buf[rows, cols] + gate(cols) * y

    @pl.when(jnp.logical_and(f == 1, t + 1 < nt))
    def _prefetch_next_tile():
        x_copy(t + 1).start()

    @pl.when(jnp.logical_and(f > 0, f < nf - 1))
    def _middle():
        u = up(h_ref[...])
        for n0 in range(0, d, MLP_OUT_CHUNK):
            cols = slice(n0, n0 + MLP_OUT_CHUNK)
            o_ref[:, cols] += gate(cols) * jnp.dot(u, wd_ref[:, cols], preferred_element_type=F32)

    @pl.when(f == nf - 1)
    def _last():
        for r0 in range(0, tm, MLP_ROW_CHUNK):
            rows = slice(r0, r0 + MLP_ROW_CHUNK)
            u = up(h_ref[rows, :])
            y = o_ref[rows, :] + mod_ref[5:6, :] * jnp.dot(u, wd_ref[...], preferred_element_type=F32)
            if final:
                ms = jnp.mean(y * y, axis=-1, keepdims=True)
                y = (y * lax.rsqrt(ms + EPS)) * gf_ref[...]
            o_ref[rows, :] = y


def _mlp(xs, mod, g, w_up, w_down, g_final, *, rows, tm, mod_index, final, next_weights=None):
    d, ff = w_up.shape
    tf = MLP_FF_TILE
    nt, nf = rows // tm, ff // tf
    assert nf >= 3 and tm % MLP_ROW_CHUNK == 0
    in_specs = [
        pl.BlockSpec(memory_space=pl.ANY),
        pl.BlockSpec((None, N_MOD, d), lambda t, f: (mod_index(t), 0, 0)),
        pl.BlockSpec((1, d), lambda t, f: (0, 0)),
        pl.BlockSpec((d, tf), lambda t, f: (0, f)),
        pl.BlockSpec((tf, d), lambda t, f: (f, 0)),
        pl.BlockSpec((1, d), lambda t, f: (0, 0)),
    ]
    out_specs = [pl.BlockSpec((tm, d), lambda t, f: (t, 0))]
    out_shape = [jax.ShapeDtypeStruct((rows, d), F32)]
    operands = [xs, mod, g, w_up, w_down, g_final]
    n_cast = 0
    if next_weights is not None:
        wu32, wd32, nxt = next_weights
        n_cast = MLP_CAST_STEPS
        while n_cast > nt * nf:
            n_cast //= 2
        cu, cd = d // n_cast, ff // n_cast
        assert d % n_cast == 0 and cu % (2 * SUBLANE) == 0

        def slab(t, f):
            return jnp.minimum(t * nf + f, n_cast - 1)
        in_specs += [pl.BlockSpec((None, cu, ff), lambda t, f: (nxt, slab(t, f), 0)),
                     pl.BlockSpec((None, cd, d), lambda t, f: (nxt, slab(t, f), 0))]
        out_specs += [pl.BlockSpec((cu, ff), lambda t, f: (slab(t, f), 0)),
                      pl.BlockSpec((cd, d), lambda t, f: (slab(t, f), 0))]
        out_shape += [jax.ShapeDtypeStruct((d, ff), BF16), jax.ShapeDtypeStruct((ff, d), BF16)]
        operands += [wu32, wd32]
    return pl.pallas_call(
        functools.partial(_mlp_kernel, final=final, n_cast=n_cast),
        grid=(nt, nf),
        in_specs=in_specs,
        out_specs=out_specs,
        out_shape=out_shape,
        scratch_shapes=[pltpu.VMEM((tm, d), F32), pltpu.VMEM((tm, d), BF16), pltpu.SemaphoreType.DMA(())],
        compiler_params=_params("arbitrary", "arbitrary"),
        name="mlp_final" if final else "mlp",
    )(*operands)


def _qkv_kernel(*refs, n_q, n_kv, hd, lat_tiles):
    if lat_tiles is None:
        x_ref, mod_ref, g_ref, w_ref, cq_ref, sq_ref, ck_ref, sk_ref, q_ref, k_ref, v_ref = refs
    else:
        (xl_ref, xc_ref, mod_ref, g_ref, w_ref, cq_ref, sq_ref, ck_ref, sk_ref,
         q_ref, k_ref, v_ref, xs_ref) = refs
        is_lat = pl.program_id(0) < lat_tiles
    group = 4
    for r0 in range(0, q_ref.shape[0], MLP_ROW_CHUNK):
        rows = slice(r0, r0 + MLP_ROW_CHUNK)
        if lat_tiles is None:
            x = x_ref[rows, :]
        else:
            x = jnp.where(is_lat, xl_ref[rows, :], xc_ref[rows, :])
            xs_ref[rows, :] = x
        h = _normmod(x, g_ref[...], mod_ref[0:1, :], mod_ref[1:2, :]).astype(BF16)

        def rope(y, c_ref, s_ref):
            return y * c_ref[rows, :] + pltpu.roll(y, hd // 2, axis=1) * s_ref[rows, :]

        for h0 in range(0, n_q, group):
            acc = jnp.dot(h, w_ref[:, h0 * hd:(h0 + group) * hd], preferred_element_type=F32)
            for j in range(group):
                y = acc[:, j * hd:(j + 1) * hd]
                q_ref[rows, (h0 + j) * hd:(h0 + j + 1) * hd] = rope(y, cq_ref, sq_ref).astype(BF16)
        acc = jnp.dot(h, w_ref[:, n_q * hd:(n_q + n_kv) * hd], preferred_element_type=F32)
        for j in range(n_kv):
            k_ref[rows, j * hd:(j + 1) * hd] = rope(acc[:, j * hd:(j + 1) * hd], ck_ref, sk_ref).astype(BF16)
        acc = jnp.dot(h, w_ref[:, (n_q + n_kv) * hd:(n_q + 2 * n_kv) * hd], preferred_element_type=F32)
        v_ref[rows, :] = acc.astype(BF16)


def _qkv(streams, mod, g, wqkv, tables, *, tm, n_q, n_kv, hd, mod_index, table_index):
    split = len(streams) == 2
    rows = sum(s.shape[0] for s in streams)
    d = streams[0].shape[1]
    lat_tiles = streams[0].shape[0] // tm if split else None
    tab_spec = pl.BlockSpec((None, tm, hd), lambda t: (table_index(t), 0, 0))
    if split:
        x_specs = [pl.BlockSpec((tm, d), lambda t: (jnp.minimum(t, lat_tiles - 1), 0)),
                   pl.BlockSpec((tm, d), lambda t: (jnp.maximum(t - lat_tiles, 0), 0))]
    else:
        x_specs = [pl.BlockSpec((tm, d), lambda t: (t, 0))]
    out_specs = [
        pl.BlockSpec((tm, n_q * hd), lambda t: (t, 0)),
        pl.BlockSpec((tm, n_kv * hd), lambda t: (t, 0)),
        pl.BlockSpec((tm, n_kv * hd), lambda t: (t, 0)),
    ]
    out_shape = [
        jax.ShapeDtypeStruct((rows, n_q * hd), BF16),
        jax.ShapeDtypeStruct((rows, n_kv * hd), BF16),
        jax.ShapeDtypeStruct((rows, n_kv * hd), BF16),
    ]
    if split:
        out_specs.append(pl.BlockSpec((tm, d), lambda t: (t, 0)))
        out_shape.append(jax.ShapeDtypeStruct((rows, d), F32))
    return pl.pallas_call(
        functools.partial(_qkv_kernel, n_q=n_q, n_kv=n_kv, hd=hd, lat_tiles=lat_tiles),
        grid=(rows // tm,),
        in_specs=x_specs + [
            pl.BlockSpec((None, N_MOD, d), lambda t: (mod_index(t), 0, 0)),
            pl.BlockSpec((1, d), lambda t: (0, 0)),
            pl.BlockSpec(wqkv.shape, lambda t: (0, 0), pipeline_mode=pl.Buffered(1)),
            tab_spec, tab_spec, tab_spec, tab_spec,
        ],
        out_specs=out_specs,
        out_shape=out_shape,
        compiler_params=_params("parallel"),
        name="qkv_rope_join" if split else "qkv_rope",
    )(*streams, mod, g, wqkv, *tables)


def _attn_kernel(sink_ref, q_ref, kp_ref, kc_ref, kn_ref, kx_ref, vp_ref, vc_ref, vn_ref, vx_ref,
                 bias_ref, o_ref, *, n_kv, grp, hd):
    bias = bias_ref[...]
    blk = q_ref.shape[0]
    n_keys = bias.shape[1]
    ones = jnp.ones((n_keys, hd), BF16)
    for hk in range(n_kv):
        sl = slice(hk * hd, (hk + 1) * hd)
        k_all = jnp.concatenate([kp_ref[:, sl], kc_ref[:, sl], kn_ref[:, sl], kx_ref[:, sl]], axis=0)
        v_all = jnp.concatenate([vp_ref[:, sl], vc_ref[:, sl], vn_ref[:, sl], vx_ref[:, sl]], axis=0)
        v_aug = jnp.concatenate([v_all, ones], axis=1)
        heads = [hk * grp + j for j in range(grp)]
        qh = jnp.concatenate([q_ref[:, hq * hd:(hq + 1) * hd] for hq in heads], axis=0)
        s = lax.dot_general(qh, k_all, (((1,), (1,)), ((), ())), preferred_element_type=F32) + bias
        sink = jnp.concatenate([jnp.full((blk, 1), sink_ref[hq], F32) for hq in heads], axis=0)
        m = jnp.maximum(jnp.max(s, axis=-1, keepdims=True), sink)
        p = jnp.exp2(s - m).astype(BF16)
        ov = jnp.dot(p, v_aug, preferred_element_type=F32)
        o = ov[:, :hd] / (ov[:, hd:] + jnp.exp2(sink - m))
        for j, hq in enumerate(heads):
            o_ref[:, hq * hd:(hq + 1) * hd] = o[j * blk:(j + 1) * blk, :].astype(o_ref.dtype)


def _attention(q, k, v, sink, bias, *, batch, seq, ctx_len, n_q, n_kv, hd, ctx_out):
    rows = q.shape[0]
    blk = ATT_BLOCK
    nb = seq // blk
    ncb = ctx_len // blk
    lat_blocks = batch * nb
    nq_blocks = nb + (ncb if ctx_out else 0)

    def q_idx(b, n):
        return (jnp.where(n < nb, b * nb + n, lat_blocks + b * ncb + (n - nb)), 0)

    def kv_idx(off):
        return lambda b, n: (b * nb + jnp.clip(n + off, 0, nb - 1), 0)

    def ctx_idx(b, n):
        return (lat_blocks * blk // ctx_len + b, 0)

    def bias_idx(b, n):
        return (jnp.where(n >= nb, 3, jnp.where(n == 0, 0, jnp.where(n == nb - 1, 2, 1))), 0, 0)

    kv_w = n_kv * hd
    kv_specs = [pl.BlockSpec((blk, kv_w), kv_idx(-1)), pl.BlockSpec((blk, kv_w), kv_idx(0)),
                pl.BlockSpec((blk, kv_w), kv_idx(1)), pl.BlockSpec((ctx_len, kv_w), ctx_idx)]
    return pl.pallas_call(
        functools.partial(_attn_kernel, n_kv=n_kv, grp=n_q // n_kv, hd=hd),
        grid=(batch, nq_blocks),
        in_specs=[pl.BlockSpec(memory_space=pltpu.SMEM),
                  pl.BlockSpec((blk, n_q * hd), q_idx)] + kv_specs + kv_specs
                 + [pl.BlockSpec((None,) + bias.shape[1:], bias_idx)],
        out_specs=pl.BlockSpec((blk, n_q * hd), q_idx),
        out_shape=jax.ShapeDtypeStruct((rows if ctx_out else lat_blocks * blk, n_q * hd), BF16),
        compiler_params=_params("parallel", "parallel"),
        name="window_attention",
    )(sink, q, k, k, k, k, v, v, v, v, bias)


def _proj_res_kernel(a_ref, w_ref, x_ref, mod_ref, o_ref):
    a = a_ref[...]
    for n0 in range(0, o_ref.shape[1], MLP_OUT_CHUNK):
        cols = slice(n0, n0 + MLP_OUT_CHUNK)
        y = jnp.dot(a, w_ref[:, cols], preferred_element_type=F32)
        o_ref[:, cols] = x_ref[:, cols] + mod_ref[2:3, cols] * y


def _proj_residual(a, w, xs, mod, *, layer, rows, tm, mod_index):
    _, kdim, d = w.shape
    return pl.pallas_call(
        _proj_res_kernel,
        grid=(rows // tm,),
        in_specs=[
            pl.BlockSpec((tm, kdim), lambda t: (t, 0)),
            pl.BlockSpec((None, kdim, d), lambda t: (layer, 0, 0), pipeline_mode=pl.Buffered(1)),
            pl.BlockSpec((tm, d), lambda t: (t, 0)),
            pl.BlockSpec((None, N_MOD, d), lambda t: (mod_index(t), 0, 0)),
        ],
        out_specs=pl.BlockSpec((tm, d), lambda t: (t, 0)),
        out_shape=jax.ShapeDtypeStruct(xs.shape, F32),
        input_output_aliases={2: 0},
        compiler_params=_params("parallel"),
        name="proj_residual",
    )(a, w, xs, mod)


def _rope_tables(seq, tm, hd, scale):
    q = hd // 4
    pos = jnp.arange(seq)
    row = (pos // GRID_W).astype(F32)[:, None]
    col = (pos % GRID_W).astype(F32)[:, None]
    axis_dim = hd // 2
    inv_freq = ROPE_BASE ** (-jnp.arange(0, axis_dim, 2, dtype=F32) / axis_dim)
    a_row, a_col = row * inv_freq, col * inv_freq
    cos = jnp.concatenate([jnp.cos(a_row), jnp.cos(a_col)] * 2, axis=1)
    sin = jnp.concatenate([-jnp.sin(a_row), -jnp.sin(a_col), jnp.sin(a_row), jnp.sin(a_col)], axis=1)
    assert cos.shape == (seq, 4 * q)
    cos = jnp.concatenate([cos.reshape(seq // tm, tm, hd), jnp.ones((1, tm, hd), F32)], axis=0)
    sin = jnp.concatenate([sin.reshape(seq // tm, tm, hd), jnp.zeros((1, tm, hd), F32)], axis=0)
    return cos * scale, sin * scale, cos, sin


def _head_perm(n_heads, hd):
    q = hd // 4
    one = jnp.concatenate([jnp.arange(0, q), jnp.arange(2 * q, 3 * q), jnp.arange(q, 2 * q), jnp.arange(3 * q, 4 * q)])
    return (jnp.arange(n_heads)[:, None] * hd + one[None, :]).reshape(-1)


def _attn_bias(grp, ctx_len):
    blk = ATT_BLOCK
    qi = jnp.arange(grp * blk)[:, None] % blk
    kj = jnp.arange(3 * blk + ctx_len)[None, :]
    is_ctx = kj >= 3 * blk
    win = jnp.abs(kj - blk - qi) <= ATT_BLOCK
    variants = [win & (kj >= blk), win, win & (kj < 2 * blk), jnp.zeros_like(win)]
    return jnp.stack([jnp.where(is_ctx | m, 0.0, -jnp.inf).astype(F32) for m in variants])


def _gmlp_gate_kernel(x_ref, mod_ref, g_ref, w_ref, gv_ref, ws_ref, bs_ref, z_ref,
                      h_ref, v_ref, mu_ref, rs_ref, *, n_vt, chunk, cg):
    j = pl.program_id(1)
    tm, tn = z_ref.shape

    def gelu_proj(h, cols):
        return jax.nn.gelu(jnp.dot(h, w_ref[:, cols], preferred_element_type=F32))

    def v_cols(h, rows, tile):
        tot = jnp.zeros((h.shape[0], 1), F32)
        for n0 in range(0, tn, GMLP_COL_CHUNK):
            cols = slice(n0, n0 + GMLP_COL_CHUNK)
            y = gelu_proj(h, cols)
            v_ref[tile, rows, cols] = y.astype(BF16)
            tot += jnp.sum(y, axis=-1, keepdims=True)
        return tot

    @pl.when(j == 0)
    def _first_v_tile():
        for r0 in range(0, tm, MLP_ROW_CHUNK):
            rows = slice(r0, r0 + MLP_ROW_CHUNK)
            h = _normmod(x_ref[rows, :], g_ref[...], mod_ref[0:1, :], mod_ref[1:2, :]).astype(BF16)
            h_ref[rows, :] = h
            mu_ref[rows, :] = jnp.broadcast_to(v_cols(h, rows, 0), (MLP_ROW_CHUNK, LANE))

    row_block = min(GMLP_ROW_BLOCK, tm)

    @pl.when(jnp.logical_and(j > 0, j < n_vt))
    def _v_tile():
        for rb in range(0, tm, row_block):
            rows = slice(rb, rb + row_block)
            mu_ref[rows, :] += jnp.broadcast_to(v_cols(h_ref[rows, :], rows, j), (row_block, LANE))

    @pl.when(j == n_vt)
    def _v_stats():
        width = n_vt * tn

        def body(rows):
            mu = mu_ref[rows, :] / width
            mu_wide = jnp.concatenate([mu] * (tn // LANE), axis=1)
            sq = jnp.zeros((ROW_CHUNK, 1), F32)
            for k in range(n_vt):
                dv = v_ref[k, rows, :].astype(F32) - mu_wide
                sq += jnp.sum(dv * dv, axis=-1, keepdims=True)
            mu_ref[rows, :] = mu
            rs_ref[rows, :] = jnp.broadcast_to(lax.rsqrt(sq / width + EPS), (ROW_CHUNK, LANE))
        _for_row_chunks(tm, body)

    @pl.when(j >= n_vt)
    def _u_tile():
        jj = j - n_vt
        rep = cg // LANE
        for rb in range(0, tm, row_block):
            h = h_ref[rb:rb + row_block, :]
            for n0 in range(0, tn, GMLP_COL_CHUNK):
                u = gelu_proj(h, slice(n0, n0 + GMLP_COL_CHUNK))
                for g0 in range(0, GMLP_COL_CHUNK, cg):
                    gi = jj * (tn // cg) + (n0 + g0) // cg
                    cols = slice(n0 + g0, n0 + g0 + cg)
                    w_s = ws_ref[gi]
                    b_s = jnp.concatenate([bs_ref[gi]] * rep, axis=1)
                    gv = gv_ref[jj, :, cols]
                    for r0 in range(0, row_block, chunk):
                        rows = slice(rb + r0, rb + r0 + chunk)
                        mu = jnp.concatenate([mu_ref[rows, :]] * rep, axis=1)
                        rs = jnp.concatenate([rs_ref[rows, :]] * rep, axis=1)
                        vn = ((v_ref[jj, rows, cols].astype(F32) - mu) * rs * gv).astype(BF16)
                        sv = jnp.dot(w_s, vn, preferred_element_type=F32) + b_s
                        z_ref[rows, cols] = (u[r0:r0 + chunk, g0:g0 + cg] * sv).astype(BF16)


def _gmlp_gate(xs, mod, g, w_in, g_v, w_s, b_s, *, layer, tm, mod_index):
    rows, d = xs.shape
    tn = GMLP_IN_TILE
    width = w_in.shape[2] // 2
    n_vt = width // tn
    groups, chunk = w_s.shape[1], w_s.shape[2]
    cg = width // groups
    assert tn % GMLP_COL_CHUNK == 0 and GMLP_COL_CHUNK % cg == 0 and cg % LANE == 0 and tm % chunk == 0
    g_v_tiles = g_v[layer].reshape(n_vt, 1, tn)
    b_s_wide = jnp.broadcast_to(b_s[layer][:, :, None], (groups, chunk, LANE))
    return pl.pallas_call(
        functools.partial(_gmlp_gate_kernel, n_vt=n_vt, chunk=chunk, cg=cg),
        grid=(rows // tm, 2 * n_vt),
        in_specs=[
            pl.BlockSpec((tm, d), lambda t, j: (t, 0)),
            pl.BlockSpec((None, N_MOD, d), lambda t, j: (mod_index(t), 0, 0)),
            pl.BlockSpec((1, d), lambda t, j: (0, 0)),
            pl.BlockSpec((None, d, tn), lambda t, j: (layer, 0, (j + n_vt) % (2 * n_vt))),
            pl.BlockSpec((n_vt, 1, tn), lambda t, j: (0, 0, 0)),
            pl.BlockSpec((None, groups, chunk, chunk), lambda t, j: (layer, 0, 0, 0)),
            pl.BlockSpec((groups, chunk, LANE), lambda t, j: (0, 0, 0)),
        ],
        out_specs=pl.BlockSpec((tm, tn), lambda t, j: (t, jnp.maximum(j - n_vt, 0))),
        out_shape=jax.ShapeDtypeStruct((rows, width), BF16),
        scratch_shapes=[pltpu.VMEM((tm, d), BF16), pltpu.VMEM((n_vt, tm, tn), BF16),
                        pltpu.VMEM((tm, LANE), F32), pltpu.VMEM((tm, LANE), F32)],
        compiler_params=_params("parallel", "arbitrary"),
        name="gmlp_gate",
    )(xs, mod, g, w_in, g_v_tiles, w_s, b_s_wide)


def _pool_kernel(x_ref, xp_ref, xn_ref, mod_ref, g_ref, w_ref, sc_ref, o_ref, e_ref,
                 *, tm, lat_tiles, seq, ctx_len):
    t = pl.program_id(0)
    is_lat = t < lat_tiles
    seq_len = jnp.where(is_lat, seq, ctx_len)
    tiles_per_seq = jnp.where(is_lat, seq // tm, ctx_len // tm)
    p = jnp.where(is_lat, t % (seq // tm), (t - lat_tiles) % (ctx_len // tm))
    has_prev = (p > 0).astype(F32)
    has_next = (p < tiles_per_seq - 1).astype(F32)

    g, sh, sc = g_ref[...], mod_ref[0:1, :], mod_ref[1:2, :]
    x = x_ref[...]
    halo = POOL_HALO
    e_ref[0:halo, :] = _normmod(xp_ref[...], g, sh, sc) * has_prev
    e_ref[halo:halo + tm, :] = _normmod(x, g, sh, sc)
    e_ref[halo + tm:2 * halo + tm, :] = _normmod(xn_ref[...], g, sh, sc) * has_next

    pos = p * tm + lax.broadcasted_iota(jnp.int32, (tm, 1), 0)
    n_ext = tm + 2 * halo
    cgw = x.shape[1] // len(POOL_SIZES)
    for gi, w in enumerate(POOL_SIZES):
        cols = slice(gi * cgw, (gi + 1) * cgw)
        e = e_ref[:, cols]
        a = e + pltpu.roll(e, 1, axis=0)
        step = 1
        while 2 * step < w:
            a = pltpu.roll(a, step, axis=0) + pltpu.roll(a, n_ext - step, axis=0)
            step *= 2
        cnt = (jnp.minimum(pos + w // 2, seq_len) - jnp.maximum(pos - w // 2, 0)).astype(F32)
        dlt = a[halo:halo + tm, :] / cnt - e[halo:halo + tm, :]
        y = jnp.dot(dlt.astype(BF16), w_ref[gi], preferred_element_type=F32) * sc_ref[:, cols]
        o_ref[:, cols] = x[:, cols] + mod_ref[2:3, cols] * y


def _pool_residual(xs, mod, g, w_pool, scale, *, tm, lat_rows, seq, ctx_len, mod_index):
    rows, d = xs.shape
    hb = tm // POOL_HALO
    n_hblocks = rows // POOL_HALO
    return pl.pallas_call(
        functools.partial(_pool_kernel, tm=tm, lat_tiles=lat_rows // tm, seq=seq, ctx_len=ctx_len),
        grid=(rows // tm,),
        in_specs=[
            pl.BlockSpec((tm, d), lambda t: (t, 0)),
            pl.BlockSpec((POOL_HALO, d), lambda t: (jnp.maximum(t * hb - 1, 0), 0)),
            pl.BlockSpec((POOL_HALO, d), lambda t: (jnp.minimum((t + 1) * hb, n_hblocks - 1), 0)),
            pl.BlockSpec((None, N_MOD, d), lambda t: (mod_index(t), 0, 0)),
            pl.BlockSpec((1, d), lambda t: (0, 0)),
            pl.BlockSpec(w_pool.shape, lambda t: (0, 0, 0)),
            pl.BlockSpec((1, d), lambda t: (0, 0)),
        ],
        out_specs=pl.BlockSpec((tm, d), lambda t: (t, 0)),
        out_shape=jax.ShapeDtypeStruct(xs.shape, F32),
        scratch_shapes=[pltpu.VMEM((tm + 2 * POOL_HALO, d), F32)],
        compiler_params=_params("parallel"),
        name="pool_mixer",
    )(xs, xs, xs, mod, g, w_pool, scale)


def kernel(x, c, ctx, c_ctx, w_mod, b_mod, g_norm, w_up, w_down, attn_wq, attn_wk, attn_wv, attn_wo, attn_sink, gmlp_w_in, gmlp_g_v, gmlp_w_s, gmlp_b_s, gmlp_w_out, pool_w, pool_scale, g_final):
    B, S, D = x.shape
    L = ctx.shape[1]
    depth = w_mod.shape[0]
    n_q = attn_sink.shape[1]
    hd = attn_wq.shape[2] // n_q
    n_kv = attn_wk.shape[2] // hd
    lat_rows, ctx_rows = B * S, B * L

    tm = min(1024, S, ctx_rows)
    tm_qkv = min(512, S, ctx_rows)
    tm_proj = min(512, S, ctx_rows)
    tm_pool = min(256, S, L)
    assert S % tm == 0 and ctx_rows % tm == 0 and S % ATT_BLOCK == 0 and L % ATT_BLOCK == 0
    assert lat_rows % L == 0 and S // ATT_BLOCK >= 2 and S % GRID_W == 0

    def mod_index_for(tile):
        return lambda t: jnp.minimum(t // (S // tile), B)

    assert N_MIXERS == 3 and depth >= 1
    xs = None

    n_cond = -(-(B + 1) // SUBLANE) * SUBLANE
    cond = jnp.concatenate([c, c_ctx[None, :], jnp.zeros((n_cond - B - 1, D), F32)], axis=0)
    mods = _modulation(cond, w_mod, b_mod).reshape(depth, n_cond, N_MOD, D)

    tables = _rope_tables(S, tm_qkv, hd, hd ** -0.5 * LOG2_E)
    bias = _attn_bias(n_q // n_kv, L)
    qperm, kperm = _head_perm(n_q, hd), _head_perm(n_kv, hd)

    def table_index(t):
        return jnp.where(t < lat_rows // tm_qkv, t % (S // tm_qkv), S // tm_qkv)

    w_up_bf, w_down_bf = w_up[0].astype(BF16), w_down[0].astype(BF16)
    attn_wo_bf = attn_wo.astype(BF16)
    gmlp_w_in_bf, gmlp_w_s_bf, gmlp_w_out_bf = gmlp_w_in.astype(BF16), gmlp_w_s.astype(BF16), gmlp_w_out.astype(BF16)

    out = None
    for i in range(depth):
        kind, j = i % N_MIXERS, i // N_MIXERS
        last = i == depth - 1
        mod = mods[i]
        g_a, g_m = g_norm[i, 0][None, :], g_norm[i, 1][None, :]
        res_rows = lat_rows if last else lat_rows + ctx_rows

        if kind == 0:
            wqkv = jnp.concatenate([attn_wq[j][:, qperm], attn_wk[j][:, kperm], attn_wv[j]], axis=1).astype(BF16)
            streams = (x.reshape(lat_rows, D), ctx.reshape(ctx_rows, D)) if i == 0 else (xs,)
            qkv_out = _qkv(streams, mod, g_a, wqkv, tables, tm=tm_qkv, n_q=n_q, n_kv=n_kv, hd=hd,
                           mod_index=mod_index_for(tm_qkv), table_index=table_index)
            q, k, v = qkv_out[:3]
            if i == 0:
                xs = qkv_out[3]
            o = _attention(q, k, v, attn_sink[j] * LOG2_E, bias, batch=B, seq=S, ctx_len=L,
                           n_q=n_q, n_kv=n_kv, hd=hd, ctx_out=not last)
            xs = _proj_residual(o, attn_wo_bf, xs, mod, layer=j, rows=res_rows, tm=tm_proj,
                                mod_index=mod_index_for(tm_proj))
        elif kind == 1:
            z = _gmlp_gate(xs, mod, g_a, gmlp_w_in_bf, gmlp_g_v, gmlp_w_s_bf, gmlp_b_s,
                           layer=j, tm=tm, mod_index=mod_index_for(tm))
            xs = _proj_residual(z, gmlp_w_out_bf, xs, mod, layer=j, rows=res_rows, tm=tm_proj,
                                mod_index=mod_index_for(tm_proj))
        else:
            xs = _pool_residual(xs, mod, g_a, pool_w[j].astype(BF16), pool_scale[j][None, :],
                                tm=tm_pool, lat_rows=lat_rows, seq=S, ctx_len=L,
                                mod_index=mod_index_for(tm_pool))

        res = _mlp(xs, mod, g_m, w_up_bf, w_down_bf, g_final[None, :], rows=res_rows, tm=tm,
                   mod_index=mod_index_for(tm), final=last,
                   next_weights=None if last else (w_up, w_down, i + 1))
        if last:
            out = res[0]
        else:
            xs, w_up_bf, w_down_bf = res
    return out.reshape(B, S, D)
```

```python
import functools

import jax
import jax.numpy as jnp
from jax import lax
from jax.experimental import pallas as pl
from jax.experimental.pallas import tpu as pltpu

EPS = 1e-6
N_MOD = 6
N_MIXERS = 3
GRID_W = 64
ATT_BLOCK = 128
ROPE_BASE = 10000.0
LOG2_E = 1.4426950408889634
POOL_SIZES = (2, 4, 8, 16)
POOL_HALO = 8
LANE = 128
SUBLANE = 8
VMEM_LIMIT_BYTES = 56 * 1024 * 1024

BF16 = jnp.bfloat16
F32 = jnp.float32


def _params(*sem):
    return pltpu.CompilerParams(dimension_semantics=sem, vmem_limit_bytes=VMEM_LIMIT_BYTES)


def _normmod(x, g, shift, scale):
    ms = jnp.mean(x * x, axis=-1, keepdims=True)
    return (x * lax.rsqrt(ms + EPS)) * g * (1.0 + scale) + shift


ROW_CHUNK = 128
MLP_OUT_CHUNK = 512
MLP_ROW_CHUNK = 256
GMLP_COL_CHUNK = 256
MLP_FF_TILE = 1024
GMLP_IN_TILE = 2048
GMLP_ROW_BLOCK = 512
MLP_CAST_STEPS = 128


def _for_row_chunks(n_rows, body):
    def step(i, carry):
        body(pl.ds(pl.multiple_of(i * ROW_CHUNK, ROW_CHUNK), ROW_CHUNK))
        return carry
    lax.fori_loop(0, n_rows // ROW_CHUNK, step, 0)


def _normmod_into(dst_ref, x_ref, g_ref, mod_ref, shift_row):
    def body(rows):
        h = _normmod(x_ref[rows, :], g_ref[...], mod_ref[shift_row:shift_row + 1, :],
                     mod_ref[shift_row + 1:shift_row + 2, :])
        dst_ref[rows, :] = h.astype(dst_ref.dtype)
    _for_row_chunks(x_ref.shape[0], body)


def _mod_kernel(c_ref, w_ref, b_ref, o_ref):
    c = c_ref[...]
    s = (c * jax.nn.sigmoid(c)).astype(BF16)
    o_ref[...] = jnp.dot(s, w_ref[...].astype(BF16), preferred_element_type=F32) + b_ref[...]


def _modulation(cond, w_mod, b_mod):
    depth, d, n = w_mod.shape
    r = cond.shape[0]
    tn = 1024
    return pl.pallas_call(
        _mod_kernel,
        grid=(depth, n // tn),
        in_specs=[
            pl.BlockSpec((r, d), lambda i, j: (0, 0)),
            pl.BlockSpec((None, d, tn), lambda i, j: (i, 0, j)),
            pl.BlockSpec((None, 1, tn), lambda i, j: (i, 0, j)),
        ],
        out_specs=pl.BlockSpec((None, r, tn), lambda i, j: (i, 0, j)),
        out_shape=jax.ShapeDtypeStruct((depth, r, n), F32),
        compiler_params=_params("parallel", "parallel"),
        name="modulation",
    )(cond, w_mod, b_mod.reshape(depth, 1, n))


def _mlp_kernel(*refs, final, n_cast):
    if n_cast:
        (x_hbm, mod_ref, g_ref, wu_ref, wd_ref, gf_ref, wun_ref, wdn_ref,
         o_ref, wun_o, wdn_o, x_buf, h_ref, x_sem) = refs
    else:
        x_hbm, mod_ref, g_ref, wu_ref, wd_ref, gf_ref, o_ref, x_buf, h_ref, x_sem = refs
    t, f = pl.program_id(0), pl.program_id(1)
    nt, nf = pl.num_programs(0), pl.num_programs(1)
    tm, d = o_ref.shape

    if n_cast:
        @pl.when(t * nf + f < n_cast)
        def _cast_next_layer_weights():
            wun_o[...] = wun_ref[...].astype(BF16)
            wdn_o[...] = wdn_ref[...].astype(BF16)

    def x_copy(tile):
        return pltpu.make_async_copy(x_hbm.at[pl.ds(tile * tm, tm), :], x_buf, x_sem)

    def up(h):
        u = jnp.dot(h, wu_ref[...], preferred_element_type=F32)
        return jnp.square(jnp.maximum(u, 0.0)).astype(BF16)

    def gate(cols):
        return mod_ref[5:6, cols]

    @pl.when(jnp.logical_and(t == 0, f == 0))
    def _fetch_first_tile():
        x_copy(0).start()

    @pl.when(f == 0)
    def _first():
        x_copy(t).wait()
        for r0 in range(0, tm, MLP_ROW_CHUNK):
            rows = slice(r0, r0 + MLP_ROW_CHUNK)
            h = _normmod(x_buf[rows, :], g_ref[...], mod_ref[3:4, :], mod_ref[4:5, :]).astype(BF16)
            h_ref[rows, :] = h
            u = up(h)
            for n0 in range(0, d, MLP_OUT_CHUNK):
                cols = slice(n0, n0 + MLP_OUT_CHUNK)
                y = jnp.dot(u, wd_ref[:, cols], preferred_element_type=F32)
                o_ref[rows, cols] = x_buf[rows, cols] + gate(cols) * y

    @pl.when(jnp.logical_and(f == 1, t + 1 < nt))
    def _prefetch_next_tile():
        x_copy(t + 1).start()

    @pl.when(jnp.logical_and(f > 0, f < nf - 1))
    def _middle():
        u = up(h_ref[...])
        for n0 in range(0, d, MLP_OUT_CHUNK):
            cols = slice(n0, n0 + MLP_OUT_CHUNK)
            o_ref[:, cols] += gate(cols) * jnp.dot(u, wd_ref[:, cols], preferred_element_type=F32)

    @pl.when(f == nf - 1)
    def _last():
        for r0 in range(0, tm, MLP_ROW_CHUNK):
            rows = slice(r0, r0 + MLP_ROW_CHUNK)
            u = up(h_ref[rows, :])
            y = o_ref[rows, :] + mod_ref[5:6, :] * jnp.dot(u, wd_ref[...], preferred_element_type=F32)
            if final:
                ms = jnp.mean(y * y, axis=-1, keepdims=True)
                y = (y * lax.rsqrt(ms + EPS)) * gf_ref[...]
            o_ref[rows, :] = y


def _mlp(xs, mod, g, w_up, w_down, g_final, *, rows, tm, mod_index, final, next_weights=None):
    d, ff = w_up.shape
    tf = MLP_FF_TILE
    nt, nf = rows // tm, ff // tf
    assert nf >= 3 and tm % MLP_ROW_CHUNK == 0
    in_specs = [
        pl.BlockSpec(memory_space=pl.ANY),
        pl.BlockSpec((None, N_MOD, d), lambda t, f: (mod_index(t), 0, 0)),
        pl.BlockSpec((1, d), lambda t, f: (0, 0)),
        pl.BlockSpec((d, tf), lambda t, f: (0, f)),
        pl.BlockSpec((tf, d), lambda t, f: (f, 0)),
        pl.BlockSpec((1, d), lambda t, f: (0, 0)),
    ]
    out_specs = [pl.BlockSpec((tm, d), lambda t, f: (t, 0))]
    out_shape = [jax.ShapeDtypeStruct((rows, d), F32)]
    operands = [xs, mod, g, w_up, w_down, g_final]
    n_cast = 0
    if next_weights is not None:
        wu32, wd32, nxt = next_weights
        n_cast = MLP_CAST_STEPS
        while n_cast > nt * nf:
            n_cast //= 2
        cu, cd = d // n_cast, ff // n_cast
        assert d % n_cast == 0 and cu % (2 * SUBLANE) == 0

        def slab(t, f):
            return jnp.minimum(t * nf + f, n_cast - 1)
        in_specs += [pl.BlockSpec((None, cu, ff), lambda t, f: (nxt, slab(t, f), 0)),
                     pl.BlockSpec((None, cd, d), lambda t, f: (nxt, slab(t, f), 0))]
        out_specs += [pl.BlockSpec((cu, ff), lambda t, f: (slab(t, f), 0)),
                      pl.BlockSpec((cd, d), lambda t, f: (slab(t, f), 0))]
        out_shape += [jax.ShapeDtypeStruct((d, ff), BF16), jax.ShapeDtypeStruct((ff, d), BF16)]
        operands += [wu32, wd32]
    return pl.pallas_call(
        functools.partial(_mlp_kernel, final=final, n_cast=n_cast),
        grid=(nt, nf),
        in_specs=in_specs,
        out_specs=out_specs,
        out_shape=out_shape,
        scratch_shapes=[pltpu.VMEM((tm, d), F32), pltpu.VMEM((tm, d), BF16), pltpu.SemaphoreType.DMA(())],
        compiler_params=_params("arbitrary", "arbitrary"),
        name="mlp_final" if final else "mlp",
    )(*operands)


def _qkv_kernel(*refs, n_q, n_kv, hd, lat_tiles):
    if lat_tiles is None:
        x_ref, mod_ref, g_ref, w_ref, cq_ref, sq_ref, ck_ref, sk_ref, q_ref, k_ref, v_ref = refs
    else:
        (xl_ref, xc_ref, mod_ref, g_ref, w_ref, cq_ref, sq_ref, ck_ref, sk_ref,
         q_ref, k_ref, v_ref, xs_ref) = refs
        is_lat = pl.program_id(0) < lat_tiles
    group = 4
    for r0 in range(0, q_ref.shape[0], MLP_ROW_CHUNK):
        rows = slice(r0, r0 + MLP_ROW_CHUNK)
        if lat_tiles is None:
            x = x_ref[rows, :]
        else:
            x = jnp.where(is_lat, xl_ref[rows, :], xc_ref[rows, :])
            xs_ref[rows, :] = x
        h = _normmod(x, g_ref[...], mod_ref[0:1, :], mod_ref[1:2, :]).astype(BF16)

        def rope(y, c_ref, s_ref):
            return y * c_ref[rows, :] + pltpu.roll(y, hd // 2, axis=1) * s_ref[rows, :]

        for h0 in range(0, n_q, group):
            acc = jnp.dot(h, w_ref[:, h0 * hd:(h0 + group) * hd], preferred_element_type=F32)
            for j in range(group):
                y = acc[:, j * hd:(j + 1) * hd]
                q_ref[rows, (h0 + j) * hd:(h0 + j + 1) * hd] = rope(y, cq_ref, sq_ref).astype(BF16)
        acc = jnp.dot(h, w_ref[:, n_q * hd:(n_q + n_kv) * hd], preferred_element_type=F32)
        for j in range(n_kv):
            k_ref[rows, j * hd:(j + 1) * hd] = rope(acc[:, j * hd:(j + 1) * hd], ck_ref, sk_ref).astype(BF16)
        acc = jnp.dot(h, w_ref[:, (n_q + n_kv) * hd:(n_q + 2 * n_kv) * hd], preferred_element_type=F32)
        v_ref[rows, :] = acc.astype(BF16)


def _qkv(streams, mod, g, wqkv, tables, *, tm, n_q, n_kv, hd, mod_index, table_index):
    split = len(streams) == 2
    rows = sum(s.shape[0] for s in streams)
    d = streams[0].shape[1]
    lat_tiles = streams[0].shape[0] // tm if split else None
    tab_spec = pl.BlockSpec((None, tm, hd), lambda t: (table_index(t), 0, 0))
    if split:
        x_specs = [pl.BlockSpec((tm, d), lambda t: (jnp.minimum(t, lat_tiles - 1), 0)),
                   pl.BlockSpec((tm, d), lambda t: (jnp.maximum(t - lat_tiles, 0), 0))]
    else:
        x_specs = [pl.BlockSpec((tm, d), lambda t: (t, 0))]
    out_specs = [
        pl.BlockSpec((tm, n_q * hd), lambda t: (t, 0)),
        pl.BlockSpec((tm, n_kv * hd), lambda t: (t, 0)),
        pl.BlockSpec((tm, n_kv * hd), lambda t: (t, 0)),
    ]
    out_shape = [
        jax.ShapeDtypeStruct((rows, n_q * hd), BF16),
        jax.ShapeDtypeStruct((rows, n_kv * hd), BF16),
        jax.ShapeDtypeStruct((rows, n_kv * hd), BF16),
    ]
    if split:
        out_specs.append(pl.BlockSpec((tm, d), lambda t: (t, 0)))
        out_shape.append(jax.ShapeDtypeStruct((rows, d), F32))
    return pl.pallas_call(
        functools.partial(_qkv_kernel, n_q=n_q, n_kv=n_kv, hd=hd, lat_tiles=lat_tiles),
        grid=(rows // tm,),
        in_specs=x_specs + [
            pl.BlockSpec((None, N_MOD, d), lambda t: (mod_index(t), 0, 0)),
            pl.BlockSpec((1, d), lambda t: (0, 0)),
            pl.BlockSpec(wqkv.shape, lambda t: (0, 0), pipeline_mode=pl.Buffered(1)),
            tab_spec, tab_spec, tab_spec, tab_spec,
        ],
        out_specs=out_specs,
        out_shape=out_shape,
        compiler_params=_params("parallel"),
        name="qkv_rope_join" if split else "qkv_rope",
    )(*streams, mod, g, wqkv, *tables)


def _attn_kernel(sink_ref, q_ref, kp_ref, kc_ref, kn_ref, kx_ref, vp_ref, vc_ref, vn_ref, vx_ref,
                 bias_ref, o_ref, *, n_kv, grp, hd):
    bias = bias_ref[...]
    blk = q_ref.shape[0]
    n_keys = bias.shape[1]
    ones = jnp.ones((n_keys, hd), BF16)
    for hk in range(n_kv):
        sl = slice(hk * hd, (hk + 1) * hd)
        k_all = jnp.concatenate([kp_ref[:, sl], kc_ref[:, sl], kn_ref[:, sl], kx_ref[:, sl]], axis=0)
        v_all = jnp.concatenate([vp_ref[:, sl], vc_ref[:, sl], vn_ref[:, sl], vx_ref[:, sl]], axis=0)
        v_aug = jnp.concatenate([v_all, ones], axis=1)
        heads = [hk * grp + j for j in range(grp)]
        qh = jnp.concatenate([q_ref[:, hq * hd:(hq + 1) * hd] for hq in heads], axis=0)
        s = lax.dot_general(qh, k_all, (((1,), (1,)), ((), ())), preferred_element_type=F32) + bias
        sink = jnp.concatenate([jnp.full((blk, 1), sink_ref[hq], F32) for hq in heads], axis=0)
        m = jnp.maximum(jnp.max(s, axis=-1, keepdims=True), sink)
        p = jnp.exp2(s - m).astype(BF16)
        ov = jnp.dot(p, v_aug, preferred_element_type=F32)
        o = ov[:, :hd] / (ov[:, hd:] + jnp.exp2(sink - m))
        for j, hq in enumerate(heads):
            o_ref[:, hq * hd:(hq + 1) * hd] = o[j * blk:(j + 1) * blk, :].astype(o_ref.dtype)


def _attention(q, k, v, sink, bias, *, batch, seq, ctx_len, n_q, n_kv, hd, ctx_out):
    rows = q.shape[0]
    blk = ATT_BLOCK
    nb = seq // blk
    ncb = ctx_len // blk
    lat_blocks = batch * nb
    nq_blocks = nb + (ncb if ctx_out else 0)

    def q_idx(b, n):
        return (jnp.where(n < nb, b * nb + n, lat_blocks + b * ncb + (n - nb)), 0)

    def kv_idx(off):
        return lambda b, n: (b * nb + jnp.clip(n + off, 0, nb - 1), 0)

    def ctx_idx(b, n):
        return (lat_blocks * blk // ctx_len + b, 0)

    def bias_idx(b, n):
        return (jnp.where(n >= nb, 3, jnp.where(n == 0, 0, jnp.where(n == nb - 1, 2, 1))), 0, 0)

    kv_w = n_kv * hd
    kv_specs = [pl.BlockSpec((blk, kv_w), kv_idx(-1)), pl.BlockSpec((blk, kv_w), kv_idx(0)),
                pl.BlockSpec((blk, kv_w), kv_idx(1)), pl.BlockSpec((ctx_len, kv_w), ctx_idx)]
    return pl.pallas_call(
        functools.partial(_attn_kernel, n_kv=n_kv, grp=n_q // n_kv, hd=hd),
        grid=(batch, nq_blocks),
        in_specs=[pl.BlockSpec(memory_space=pltpu.SMEM),
                  pl.BlockSpec((blk, n_q * hd), q_idx)] + kv_specs + kv_specs
                 + [pl.BlockSpec((None,) + bias.shape[1:], bias_idx)],
        out_specs=pl.BlockSpec((blk, n_q * hd), q_idx),
        out_shape=jax.ShapeDtypeStruct((rows if ctx_out else lat_blocks * blk, n_q * hd), BF16),
        compiler_params=_params("parallel", "parallel"),
        name="window_attention",
    )(sink, q, k, k, k, k, v, v, v, v, bias)


def _proj_res_kernel(a_ref, w_ref, x_ref, mod_ref, o_ref):
    a = a_ref[...]
    for n0 in range(0, o_ref.shape[1], MLP_OUT_CHUNK):
        cols = slice(n0, n0 + MLP_OUT_CHUNK)
        y = jnp.dot(a, w_ref[:, cols], preferred_element_type=F32)
        o_ref[:, cols] = x_ref[:, cols] + mod_ref[2:3, cols] * y


def _proj_residual(a, w, xs, mod, *, layer, rows, tm, mod_index):
    _, kdim, d = w.shape
    return pl.pallas_call(
        _proj_res_kernel,
        grid=(rows // tm,),
        in_specs=[
            pl.BlockSpec((tm, kdim), lambda t: (t, 0)),
            pl.BlockSpec((None, kdim, d), lambda t: (layer, 0, 0), pipeline_mode=pl.Buffered(1)),
            pl.BlockSpec((tm, d), lambda t: (t, 0)),
            pl.BlockSpec((None, N_MOD, d), lambda t: (mod_index(t), 0, 0)),
        ],
        out_specs=pl.BlockSpec((tm, d), lambda t: (t, 0)),
        out_shape=jax.ShapeDtypeStruct(xs.shape, F32),
        input_output_aliases={2: 0},
        compiler_params=_params("parallel"),
        name="proj_residual",
    )(a, w, xs, mod)


def _rope_tables(seq, tm, hd, scale):
    q = hd // 4
    pos = jnp.arange(seq)
    row = (pos // GRID_W).astype(F32)[:, None]
    col = (pos % GRID_W).astype(F32)[:, None]
    axis_dim = hd // 2
    inv_freq = ROPE_BASE ** (-jnp.arange(0, axis_dim, 2, dtype=F32) / axis_dim)
    a_row, a_col = row * inv_freq, col * inv_freq
    cos = jnp.concatenate([jnp.cos(a_row), jnp.cos(a_col)] * 2, axis=1)
    sin = jnp.concatenate([-jnp.sin(a_row), -jnp.sin(a_col), jnp.sin(a_row), jnp.sin(a_col)], axis=1)
    assert cos.shape == (seq, 4 * q)
    cos = jnp.concatenate([cos.reshape(seq // tm, tm, hd), jnp.ones((1, tm, hd), F32)], axis=0)
    sin = jnp.concatenate([sin.reshape(seq // tm, tm, hd), jnp.zeros((1, tm, hd), F32)], axis=0)
    return cos * scale, sin * scale, cos, sin


def _head_perm(n_heads, hd):
    q = hd // 4
    one = jnp.concatenate([jnp.arange(0, q), jnp.arange(2 * q, 3 * q), jnp.arange(q, 2 * q), jnp.arange(3 * q, 4 * q)])
    return (jnp.arange(n_heads)[:, None] * hd + one[None, :]).reshape(-1)


def _attn_bias(grp, ctx_len):
    blk = ATT_BLOCK
    qi = jnp.arange(grp * blk)[:, None] % blk
    kj = jnp.arange(3 * blk + ctx_len)[None, :]
    is_ctx = kj >= 3 * blk
    win = jnp.abs(kj - blk - qi) <= ATT_BLOCK
    variants = [win & (kj >= blk), win, win & (kj < 2 * blk), jnp.zeros_like(win)]
    return jnp.stack([jnp.where(is_ctx | m, 0.0, -jnp.inf).astype(F32) for m in variants])


def _gmlp_gate_kernel(x_hbm, mod_ref, g_ref, w_ref, gv_ref, ws_ref, bs_ref, z_ref,
                      x_buf, h_ref, v_ref, mu_ref, rs_ref, x_sem, *, n_vt, chunk, cg):
    t, j = pl.program_id(0), pl.program_id(1)
    nt = pl.num_programs(0)
    tm, tn = z_ref.shape
    width = n_vt * tn
    row_block = min(GMLP_ROW_BLOCK, tm)

    def x_copy(tile):
        return pltpu.make_async_copy(x_hbm.at[pl.ds(tile * tm, tm), :], x_buf, x_sem)

    def gelu_proj(h, cols):
        return jax.nn.gelu(jnp.dot(h, w_ref[:, cols], preferred_element_type=F32))

    def v_cols(h, rows, tile):
        tot = jnp.zeros((h.shape[0], 1), F32)
        for n0 in range(0, tn, GMLP_COL_CHUNK):
            cols = slice(n0, n0 + GMLP_COL_CHUNK)
            y = gelu_proj(h, cols)
            v_ref[tile, rows, cols] = y.astype(BF16)
            tot += jnp.sum(y, axis=-1, keepdims=True)
        return tot

    def v_stats():
        for r0 in range(0, tm, ROW_CHUNK):
            rows = slice(r0, r0 + ROW_CHUNK)
            mu = mu_ref[rows, :] / width
            mu_wide = jnp.concatenate([mu] * (tn // LANE), axis=1)
            sq = jnp.zeros((ROW_CHUNK, 1), F32)
            for k in range(n_vt):
                dv = v_ref[k, rows, :].astype(F32) - mu_wide
                sq += jnp.sum(dv * dv, axis=-1, keepdims=True)
            mu_ref[rows, :] = mu
            rs_ref[rows, :] = jnp.broadcast_to(lax.rsqrt(sq / width + EPS), (ROW_CHUNK, LANE))

    def u_tile(jj):
        rep = cg // LANE
        for rb in range(0, tm, row_block):
            h = h_ref[rb:rb + row_block, :]
            for n0 in range(0, tn, GMLP_COL_CHUNK):
                u = gelu_proj(h, slice(n0, n0 + GMLP_COL_CHUNK))
                for g0 in range(0, GMLP_COL_CHUNK, cg):
                    gi = jj * (tn // cg) + (n0 + g0) // cg
                    cols = slice(n0 + g0, n0 + g0 + cg)
                    w_s = ws_ref[gi]
                    b_s = jnp.concatenate([bs_ref[gi]] * rep, axis=1)
                    gv = gv_ref[jj, :, cols]
                    for r0 in range(0, row_block, chunk):
                        rows = slice(rb + r0, rb + r0 + chunk)
                        mu = jnp.concatenate([mu_ref[rows, :]] * rep, axis=1)
                        rs = jnp.concatenate([rs_ref[rows, :]] * rep, axis=1)
                        vn = ((v_ref[jj, rows, cols].astype(F32) - mu) * rs * gv).astype(BF16)
                        sv = jnp.dot(w_s, vn, preferred_element_type=F32) + b_s
                        z_ref[rows, cols] = (u[r0:r0 + chunk, g0:g0 + cg] * sv).astype(BF16)

    @pl.when(jnp.logical_and(t == 0, j == 0))
    def _fetch_first_tile():
        x_copy(0).start()

    @pl.when(j == 0)
    def _first_v_tile():
        x_copy(t).wait()
        for r0 in range(0, tm, MLP_ROW_CHUNK):
            rows = slice(r0, r0 + MLP_ROW_CHUNK)
            h = _normmod(x_buf[rows, :], g_ref[...], mod_ref[0:1, :], mod_ref[1:2, :]).astype(BF16)
            h_ref[rows, :] = h
            mu_ref[rows, :] = jnp.broadcast_to(v_cols(h, rows, 0), (MLP_ROW_CHUNK, LANE))

    @pl.when(jnp.logical_and(j == 1, t + 1 < nt))
    def _prefetch_next_tile():
        x_copy(t + 1).start()

    @pl.when(jnp.logical_and(j > 0, j < n_vt))
    def _v_tile():
        for rb in range(0, tm, row_block):
            rows = slice(rb, rb + row_block)
            mu_ref[rows, :] += jnp.broadcast_to(v_cols(h_ref[rows, :], rows, j), (row_block, LANE))

    @pl.when(j == n_vt)
    def _first_u_tile():
        v_stats()
        u_tile(0)

    @pl.when(j > n_vt)
    def _u_tile():
        u_tile(j - n_vt)


def _gmlp_gate(xs, mod, g, w_in, g_v, w_s, b_s, *, layer, tm, mod_index):
    rows, d = xs.shape
    tn = GMLP_IN_TILE
    width = w_in.shape[2] // 2
    n_vt = width // tn
    groups, chunk = w_s.shape[1], w_s.shape[2]
    cg = width // groups
    assert tn % GMLP_COL_CHUNK == 0 and GMLP_COL_CHUNK % cg == 0 and cg % LANE == 0 and tm % chunk == 0
    assert n_vt >= 1 and tm % MLP_ROW_CHUNK == 0
    g_v_tiles = g_v[layer].reshape(n_vt, 1, tn)
    b_s_wide = jnp.broadcast_to(b_s[layer][:, :, None], (groups, chunk, LANE))
    return pl.pallas_call(
        functools.partial(_gmlp_gate_kernel, n_vt=n_vt, chunk=chunk, cg=cg),
        grid=(rows // tm, 2 * n_vt),
        in_specs=[
            pl.BlockSpec(memory_space=pl.ANY),
            pl.BlockSpec((None, N_MOD, d), lambda t, j: (mod_index(t), 0, 0)),
            pl.BlockSpec((1, d), lambda t, j: (0, 0)),
            pl.BlockSpec((None, d, tn), lambda t, j: (layer, 0, (j + n_vt) % (2 * n_vt))),
            pl.BlockSpec((n_vt, 1, tn), lambda t, j: (0, 0, 0)),
            pl.BlockSpec((None, groups, chunk, chunk), lambda t, j: (layer, 0, 0, 0)),
            pl.BlockSpec((groups, chunk, LANE), lambda t, j: (0, 0, 0)),
        ],
        out_specs=pl.BlockSpec((tm, tn), lambda t, j: (t, jnp.maximum(j - n_vt, 0))),
        out_shape=jax.ShapeDtypeStruct((rows, width), BF16),
        scratch_shapes=[pltpu.VMEM((tm, d), F32), pltpu.VMEM((tm, d), BF16),
                        pltpu.VMEM((n_vt, tm, tn), BF16),
                        pltpu.VMEM((tm, LANE), F32), pltpu.VMEM((tm, LANE), F32),
                        pltpu.SemaphoreType.DMA(())],
        compiler_params=_params("arbitrary", "arbitrary"),
        name="gmlp_gate",
    )(xs, mod, g, w_in, g_v_tiles, w_s, b_s_wide)


def _pool_kernel(x_ref, xp_ref, xn_ref, mod_ref, g_ref, w_ref, sc_ref, o_ref, e_ref,
                 *, tm, lat_tiles, seq, ctx_len):
    t = pl.program_id(0)
    is_lat = t < lat_tiles
    seq_len = jnp.where(is_lat, seq, ctx_len)
    tiles_per_seq = jnp.where(is_lat, seq // tm, ctx_len // tm)
    p = jnp.where(is_lat, t % (seq // tm), (t - lat_tiles) % (ctx_len // tm))
    has_prev = (p > 0).astype(F32)
    has_next = (p < tiles_per_seq - 1).astype(F32)

    g, sh, sc = g_ref[...], mod_ref[0:1, :], mod_ref[1:2, :]
    x = x_ref[...]
    halo = POOL_HALO
    e_ref[0:halo, :] = _normmod(xp_ref[...], g, sh, sc) * has_prev
    e_ref[halo:halo + tm, :] = _normmod(x, g, sh, sc)
    e_ref[halo + tm:2 * halo + tm, :] = _normmod(xn_ref[...], g, sh, sc) * has_next

    pos = p * tm + lax.broadcasted_iota(jnp.int32, (tm, 1), 0)
    n_ext = tm + 2 * halo
    cgw = x.shape[1] // len(POOL_SIZES)
    for gi, w in enumerate(POOL_SIZES):
        cols = slice(gi * cgw, (gi + 1) * cgw)
        e = e_ref[:, cols]
        a = e + pltpu.roll(e, 1, axis=0)
        step = 1
        while 2 * step < w:
            a = pltpu.roll(a, step, axis=0) + pltpu.roll(a, n_ext - step, axis=0)
            step *= 2
        cnt = (jnp.minimum(pos + w // 2, seq_len) - jnp.maximum(pos - w // 2, 0)).astype(F32)
        dlt = a[halo:halo + tm, :] / cnt - e[halo:halo + tm, :]
        y = jnp.dot(dlt.astype(BF16), w_ref[gi], preferred_element_type=F32) * sc_ref[:, cols]
        o_ref[:, cols] = x[:, cols] + mod_ref[2:3, cols] * y


def _pool_residual(xs, mod, g, w_pool, scale, *, tm, lat_rows, seq, ctx_len, mod_index):
    rows, d = xs.shape
    hb = tm // POOL_HALO
    n_hblocks = rows // POOL_HALO
    return pl.pallas_call(
        functools.partial(_pool_kernel, tm=tm, lat_tiles=lat_rows // tm, seq=seq, ctx_len=ctx_len),
        grid=(rows // tm,),
        in_specs=[
            pl.BlockSpec((tm, d), lambda t: (t, 0)),
            pl.BlockSpec((POOL_HALO, d), lambda t: (jnp.maximum(t * hb - 1, 0), 0)),
            pl.BlockSpec((POOL_HALO, d), lambda t: (jnp.minimum((t + 1) * hb, n_hblocks - 1), 0)),
            pl.BlockSpec((None, N_MOD, d), lambda t: (mod_index(t), 0, 0)),
            pl.BlockSpec((1, d), lambda t: (0, 0)),
            pl.BlockSpec(w_pool.shape, lambda t: (0, 0, 0)),
            pl.BlockSpec((1, d), lambda t: (0, 0)),
        ],
        out_specs=pl.BlockSpec((tm, d), lambda t: (t, 0)),
        out_shape=jax.ShapeDtypeStruct(xs.shape, F32),
        scratch_shapes=[pltpu.VMEM((tm + 2 * POOL_HALO, d), F32)],
        compiler_params=_params("parallel"),
        name="pool_mixer",
    )(xs, xs, xs, mod, g, w_pool, scale)


def kernel(x, c, ctx, c_ctx, w_mod, b_mod, g_norm, w_up, w_down, attn_wq, attn_wk, attn_wv, attn_wo, attn_sink, gmlp_w_in, gmlp_g_v, gmlp_w_s, gmlp_b_s, gmlp_w_out, pool_w, pool_scale, g_final):
    B, S, D = x.shape
    L = ctx.shape[1]
    depth = w_mod.shape[0]
    n_q = attn_sink.shape[1]
    hd = attn_wq.shape[2] // n_q
    n_kv = attn_wk.shape[2] // hd
    lat_rows, ctx_rows = B * S, B * L

    tm = min(1024, S, ctx_rows)
    tm_qkv = min(512, S, ctx_rows)
    tm_proj = min(512, S, ctx_rows)
    tm_pool = min(256, S, L)
    assert S % tm == 0 and ctx_rows % tm == 0 and S % ATT_BLOCK == 0 and L % ATT_BLOCK == 0
    assert lat_rows % L == 0 and S // ATT_BLOCK >= 2 and S % GRID_W == 0

    def mod_index_for(tile):
        return lambda t: jnp.minimum(t // (S // tile), B)

    assert N_MIXERS == 3 and depth >= 1
    xs = None

    n_cond = -(-(B + 1) // SUBLANE) * SUBLANE
    cond = jnp.concatenate([c, c_ctx[None, :], jnp.zeros((n_cond - B - 1, D), F32)], axis=0)
    mods = _modulation(cond, w_mod, b_mod).reshape(depth, n_cond, N_MOD, D)

    tables = _rope_tables(S, tm_qkv, hd, hd ** -0.5 * LOG2_E)
    bias = _attn_bias(n_q // n_kv, L)
    qperm, kperm = _head_perm(n_q, hd), _head_perm(n_kv, hd)

    def table_index(t):
        return jnp.where(t < lat_rows // tm_qkv, t % (S // tm_qkv), S // tm_qkv)

    w_up_bf, w_down_bf = w_up[0].astype(BF16), w_down[0].astype(BF16)
    attn_wo_bf = attn_wo.astype(BF16)
    gmlp_w_in_bf, gmlp_w_s_bf, gmlp_w_out_bf = gmlp_w_in.astype(BF16), gmlp_w_s.astype(BF16), gmlp_w_out.astype(BF16)

    out = None
    for i in range(depth):
        kind, j = i % N_MIXERS, i // N_MIXERS
        last = i == depth - 1
        mod = mods[i]
        g_a, g_m = g_norm[i, 0][None, :], g_norm[i, 1][None, :]
        res_rows = lat_rows if last else lat_rows + ctx_rows

        if kind == 0:
            wqkv = jnp.concatenate([attn_wq[j][:, qperm], attn_wk[j][:, kperm], attn_wv[j]], axis=1).astype(BF16)
            streams = (x.reshape(lat_rows, D), ctx.reshape(ctx_rows, D)) if i == 0 else (xs,)
            qkv_out = _qkv(streams, mod, g_a, wqkv, tables, tm=tm_qkv, n_q=n_q, n_kv=n_kv, hd=hd,
                           mod_index=mod_index_for(tm_qkv), table_index=table_index)
            q, k, v = qkv_out[:3]
            if i == 0:
                xs = qkv_out[3]
            o = _attention(q, k, v, attn_sink[j] * LOG2_E, bias, batch=B, seq=S, ctx_len=L,
                           n_q=n_q, n_kv=n_kv, hd=hd, ctx_out=not last)
            xs = _proj_residual(o, attn_wo_bf, xs, mod, layer=j, rows=res_rows, tm=tm_proj,
                                mod_index=mod_index_for(tm_proj))
        elif kind == 1:
            z = _gmlp_gate(xs, mod, g_a, gmlp_w_in_bf, gmlp_g_v, gmlp_w_s_bf, gmlp_b_s,
                           layer=j, tm=tm, mod_index=mod_index_for(tm))
            xs = _proj_residual(z, gmlp_w_out_bf, xs, mod, layer=j, rows=res_rows, tm=tm_proj,
                                mod_index=mod_index_for(tm_proj))
        else:
            xs = _pool_residual(xs, mod, g_a, pool_w[j].astype(BF16), pool_scale[j][None, :],
                                tm=tm_pool, lat_rows=lat_rows, seq=S, ctx_len=L,
                                mod_index=mod_index_for(tm_pool))

        res = _mlp(xs, mod, g_m, w_up_bf, w_down_bf, g_final[None, :], rows=res_rows, tm=tm,
                   mod_index=mod_index_for(tm), final=last,
                   next_weights=None if last else (w_up, w_down, i + 1))
        if last:
            out = res[0]
        else:
            xs, w_up_bf, w_down_bf = res
    return out.reshape(B, S, D)
```

```python
import functools

import jax
import jax.numpy as jnp
from jax import lax
from jax.experimental import pallas as pl
from jax.experimental.pallas import tpu as pltpu

EPS = 1e-6
N_MOD = 6
N_MIXERS = 3
GRID_W = 64
ATT_BLOCK = 128
ROPE_BASE = 10000.0
LOG2_E = 1.4426950408889634
POOL_SIZES = (2, 4, 8, 16)
POOL_HALO = 8
LANE = 128
SUBLANE = 8
VMEM_LIMIT_BYTES = 56 * 1024 * 1024

BF16 = jnp.bfloat16
F32 = jnp.float32


def _params(*sem):
    return pltpu.CompilerParams(dimension_semantics=sem, vmem_limit_bytes=VMEM_LIMIT_BYTES)


def _normmod(x, g, shift, scale):
    ms = jnp.mean(x * x, axis=-1, keepdims=True)
    return (x * lax.rsqrt(ms + EPS)) * g * (1.0 + scale) + shift


ROW_CHUNK = 128
MLP_OUT_CHUNK = 512
MLP_ROW_CHUNK = 256
GMLP_COL_CHUNK = 256
MLP_FF_TILE = 1024
GMLP_IN_TILE = 2048
GMLP_ROW_BLOCK = 512
MLP_CAST_STEPS = 128


def _for_row_chunks(n_rows, body):
    def step(i, carry):
        body(pl.ds(pl.multiple_of(i * ROW_CHUNK, ROW_CHUNK), ROW_CHUNK))
        return carry
    lax.fori_loop(0, n_rows // ROW_CHUNK, step, 0)


def _normmod_into(dst_ref, x_ref, g_ref, mod_ref, shift_row):
    def body(rows):
        h = _normmod(x_ref[rows, :], g_ref[...], mod_ref[shift_row:shift_row + 1, :],
                     mod_ref[shift_row + 1:shift_row + 2, :])
        dst_ref[rows, :] = h.astype(dst_ref.dtype)
    _for_row_chunks(x_ref.shape[0], body)


def _mod_kernel(c_ref, w_ref, b_ref, o_ref):
    c = c_ref[...]
    s = (c * jax.nn.sigmoid(c)).astype(BF16)
    o_ref[...] = jnp.dot(s, w_ref[...].astype(BF16), preferred_element_type=F32) + b_ref[...]


def _modulation(cond, w_mod, b_mod):
    depth, d, n = w_mod.shape
    r = cond.shape[0]
    tn = 1024
    return pl.pallas_call(
        _mod_kernel,
        grid=(depth, n // tn),
        in_specs=[
            pl.BlockSpec((r, d), lambda i, j: (0, 0)),
            pl.BlockSpec((None, d, tn), lambda i, j: (i, 0, j)),
            pl.BlockSpec((None, 1, tn), lambda i, j: (i, 0, j)),
        ],
        out_specs=pl.BlockSpec((None, r, tn), lambda i, j: (i, 0, j)),
        out_shape=jax.ShapeDtypeStruct((depth, r, n), F32),
        compiler_params=_params("parallel", "parallel"),
        name="modulation",
    )(cond, w_mod, b_mod.reshape(depth, 1, n))


def _mlp_kernel(*refs, final, n_cast):
    if n_cast:
        (x_hbm, mod_ref, g_ref, wu_ref, wd_ref, gf_ref, wun_ref, wdn_ref,
         o_ref, wun_o, wdn_o, x_buf, h_ref, x_sem) = refs
    else:
        x_hbm, mod_ref, g_ref, wu_ref, wd_ref, gf_ref, o_ref, x_buf, h_ref, x_sem = refs
    t, f = pl.program_id(0), pl.program_id(1)
    nt, nf = pl.num_programs(0), pl.num_programs(1)
    tm, d = o_ref.shape

    if n_cast:
        @pl.when(t * nf + f < n_cast)
        def _cast_next_layer_weights():
            wun_o[...] = wun_ref[...].astype(BF16)
            wdn_o[...] = wdn_ref[...].astype(BF16)

    def x_copy(tile):
        return pltpu.make_async_copy(x_hbm.at[pl.ds(tile * tm, tm), :], x_buf, x_sem)

    def up(h):
        u = jnp.dot(h, wu_ref[...], preferred_element_type=F32)
        return jnp.square(jnp.maximum(u, 0.0)).astype(BF16)

    def gate(cols):
        return mod_ref[5:6, cols]

    @pl.when(jnp.logical_and(t == 0, f == 0))
    def _fetch_first_tile():
        x_copy(0).start()

    @pl.when(f == 0)
    def _first():
        x_copy(t).wait()
        for r0 in range(0, tm, MLP_ROW_CHUNK):
            rows = slice(r0, r0 + MLP_ROW_CHUNK)
            h = _normmod(x_buf[rows, :], g_ref[...], mod_ref[3:4, :], mod_ref[4:5, :]).astype(BF16)
            h_ref[rows, :] = h
            u = up(h)
            for n0 in range(0, d, MLP_OUT_CHUNK):
                cols = slice(n0, n0 + MLP_OUT_CHUNK)
                y = jnp.dot(u, wd_ref[:, cols], preferred_element_type=F32)
                o_ref[rows, cols] = x_buf[rows, cols] + gate(cols) * y

    @pl.when(jnp.logical_and(f == 1, t + 1 < nt))
    def _prefetch_next_tile():
        x_copy(t + 1).start()

    @pl.when(jnp.logical_and(f > 0, f < nf - 1))
    def _middle():
        u = up(h_ref[...])
        for n0 in range(0, d, MLP_OUT_CHUNK):
            cols = slice(n0, n0 + MLP_OUT_CHUNK)
            o_ref[:, cols] += gate(cols) * jnp.dot(u, wd_ref[:, cols], preferred_element_type=F32)

    @pl.when(f == nf - 1)
    def _last():
        for r0 in range(0, tm, MLP_ROW_CHUNK):
            rows = slice(r0, r0 + MLP_ROW_CHUNK)
            u = up(h_ref[rows, :])
            y = o_ref[rows, :] + mod_ref[5:6, :] * jnp.dot(u, wd_ref[...], preferred_element_type=F32)
            if final:
                ms = jnp.mean(y * y, axis=-1, keepdims=True)
                y = (y * lax.rsqrt(ms + EPS)) * gf_ref[...]
            o_ref[rows, :] = y


def _mlp(xs, mod, g, w_up, w_down, g_final, *, rows, tm, mod_index, final, next_weights=None):
    d, ff = w_up.shape
    tf = MLP_FF_TILE
    nt, nf = rows // tm, ff // tf
    assert nf >= 3 and tm % MLP_ROW_CHUNK == 0
    in_specs = [
        pl.BlockSpec(memory_space=pl.ANY),
        pl.BlockSpec((None, N_MOD, d), lambda t, f: (mod_index(t), 0, 0)),
        pl.BlockSpec((1, d), lambda t, f: (0, 0)),
        pl.BlockSpec((d, tf), lambda t, f: (0, f)),
        pl.BlockSpec((tf, d), lambda t, f: (f, 0)),
        pl.BlockSpec((1, d), lambda t, f: (0, 0)),
    ]
    out_specs = [pl.BlockSpec((tm, d), lambda t, f: (t, 0))]
    out_shape = [jax.ShapeDtypeStruct((rows, d), F32)]
    operands = [xs, mod, g, w_up, w_down, g_final]
    n_cast = 0
    if next_weights is not None:
        wu32, wd32, nxt = next_weights
        n_cast = MLP_CAST_STEPS
        while n_cast > nt * nf:
            n_cast //= 2
        cu, cd = d // n_cast, ff // n_cast
        assert d % n_cast == 0 and cu % (2 * SUBLANE) == 0

        def slab(t, f):
            return jnp.minimum(t * nf + f, n_cast - 1)
        in_specs += [pl.BlockSpec((None, cu, ff), lambda t, f: (nxt, slab(t, f), 0)),
                     pl.BlockSpec((None, cd, d), lambda t, f: (nxt, slab(t, f), 0))]
        out_specs += [pl.BlockSpec((cu, ff), lambda t, f: (slab(t, f), 0)),
                      pl.BlockSpec((cd, d), lambda t, f: (slab(t, f), 0))]
        out_shape += [jax.ShapeDtypeStruct((d, ff), BF16), jax.ShapeDtypeStruct((ff, d), BF16)]
        operands += [wu32, wd32]
    return pl.pallas_call(
        functools.partial(_mlp_kernel, final=final, n_cast=n_cast),
        grid=(nt, nf),
        in_specs=in_specs,
        out_specs=out_specs,
        out_shape=out_shape,
        scratch_shapes=[pltpu.VMEM((tm, d), F32), pltpu.VMEM((tm, d), BF16), pltpu.SemaphoreType.DMA(())],
        compiler_params=_params("arbitrary", "arbitrary"),
        name="mlp_final" if final else "mlp",
    )(*operands)


def _qkv_kernel(*refs, n_q, n_kv, hd, lat_tiles):
    if lat_tiles is None:
        x_ref, mod_ref, g_ref, w_ref, cq_ref, sq_ref, ck_ref, sk_ref, q_ref, k_ref, v_ref = refs
    else:
        (xl_ref, xc_ref, mod_ref, g_ref, w_ref, cq_ref, sq_ref, ck_ref, sk_ref,
         q_ref, k_ref, v_ref, xs_ref) = refs
        is_lat = pl.program_id(0) < lat_tiles
    group = 4
    for r0 in range(0, q_ref.shape[0], MLP_ROW_CHUNK):
        rows = slice(r0, r0 + MLP_ROW_CHUNK)
        if lat_tiles is None:
            x = x_ref[rows, :]
        else:
            x = jnp.where(is_lat, xl_ref[rows, :], xc_ref[rows, :])
            xs_ref[rows, :] = x
        h = _normmod(x, g_ref[...], mod_ref[0:1, :], mod_ref[1:2, :]).astype(BF16)

        def rope(y, c_ref, s_ref):
            return y * c_ref[rows, :] + pltpu.roll(y, hd // 2, axis=1) * s_ref[rows, :]

        for h0 in range(0, n_q, group):
            acc = jnp.dot(h, w_ref[:, h0 * hd:(h0 + group) * hd], preferred_element_type=F32)
            for j in range(group):
                y = acc[:, j * hd:(j + 1) * hd]
                q_ref[rows, (h0 + j) * hd:(h0 + j + 1) * hd] = rope(y, cq_ref, sq_ref).astype(BF16)
        acc = jnp.dot(h, w_ref[:, n_q * hd:(n_q + n_kv) * hd], preferred_element_type=F32)
        for j in range(n_kv):
            k_ref[rows, j * hd:(j + 1) * hd] = rope(acc[:, j * hd:(j + 1) * hd], ck_ref, sk_ref).astype(BF16)
        acc = jnp.dot(h, w_ref[:, (n_q + n_kv) * hd:(n_q + 2 * n_kv) * hd], preferred_element_type=F32)
        v_ref[rows, :] = acc.astype(BF16)


def _qkv(streams, mod, g, wqkv, tables, *, tm, n_q, n_kv, hd, mod_index, table_index):
    split = len(streams) == 2
    rows = sum(s.shape[0] for s in streams)
    d = streams[0].shape[1]
    lat_tiles = streams[0].shape[0] // tm if split else None
    tab_spec = pl.BlockSpec((None, tm, hd), lambda t: (table_index(t), 0, 0))
    if split:
        x_specs = [pl.BlockSpec((tm, d), lambda t: (jnp.minimum(t, lat_tiles - 1), 0)),
                   pl.BlockSpec((tm, d), lambda t: (jnp.maximum(t - lat_tiles, 0), 0))]
    else:
        x_specs = [pl.BlockSpec((tm, d), lambda t: (t, 0))]
    out_specs = [
        pl.BlockSpec((tm, n_q * hd), lambda t: (t, 0)),
        pl.BlockSpec((tm, n_kv * hd), lambda t: (t, 0)),
        pl.BlockSpec((tm, n_kv * hd), lambda t: (t, 0)),
    ]
    out_shape = [
        jax.ShapeDtypeStruct((rows, n_q * hd), BF16),
        jax.ShapeDtypeStruct((rows, n_kv * hd), BF16),
        jax.ShapeDtypeStruct((rows, n_kv * hd), BF16),
    ]
    if split:
        out_specs.append(pl.BlockSpec((tm, d), lambda t: (t, 0)))
        out_shape.append(jax.ShapeDtypeStruct((rows, d), F32))
    return pl.pallas_call(
        functools.partial(_qkv_kernel, n_q=n_q, n_kv=n_kv, hd=hd, lat_tiles=lat_tiles),
        grid=(rows // tm,),
        in_specs=x_specs + [
            pl.BlockSpec((None, N_MOD, d), lambda t: (mod_index(t), 0, 0)),
            pl.BlockSpec((1, d), lambda t: (0, 0)),
            pl.BlockSpec(wqkv.shape, lambda t: (0, 0), pipeline_mode=pl.Buffered(1)),
            tab_spec, tab_spec, tab_spec, tab_spec,
        ],
        out_specs=out_specs,
        out_shape=out_shape,
        compiler_params=_params("parallel"),
        name="qkv_rope_join" if split else "qkv_rope",
    )(*streams, mod, g, wqkv, *tables)


def _attn_kernel(sink_ref, q_ref, kp_ref, kc_ref, kn_ref, kx_ref, vp_ref, vc_ref, vn_ref, vx_ref,
                 bias_ref, o_ref, *, n_kv, grp, hd):
    bias = bias_ref[...]
    blk = q_ref.shape[0]
    n_keys = bias.shape[1]
    ones = jnp.ones((n_keys, hd), BF16)
    for hk in range(n_kv):
        sl = slice(hk * hd, (hk + 1) * hd)
        k_all = jnp.concatenate([kp_ref[:, sl], kc_ref[:, sl], kn_ref[:, sl], kx_ref[:, sl]], axis=0)
        v_all = jnp.concatenate([vp_ref[:, sl], vc_ref[:, sl], vn_ref[:, sl], vx_ref[:, sl]], axis=0)
        v_aug = jnp.concatenate([v_all, ones], axis=1)
        heads = [hk * grp + j for j in range(grp)]
        qh = jnp.concatenate([q_ref[:, hq * hd:(hq + 1) * hd] for hq in heads], axis=0)
        s = lax.dot_general(qh, k_all, (((1,), (1,)), ((), ())), preferred_element_type=F32) + bias
        sink = jnp.concatenate([jnp.full((blk, 1), sink_ref[hq], F32) for hq in heads], axis=0)
        m = jnp.maximum(jnp.max(s, axis=-1, keepdims=True), sink)
        p = jnp.exp2(s - m).astype(BF16)
        ov = jnp.dot(p, v_aug, preferred_element_type=F32)
        o = ov[:, :hd] / (ov[:, hd:] + jnp.exp2(sink - m))
        for j, hq in enumerate(heads):
            o_ref[:, hq * hd:(hq + 1) * hd] = o[j * blk:(j + 1) * blk, :].astype(o_ref.dtype)


def _attention(q, k, v, sink, bias, *, batch, seq, ctx_len, n_q, n_kv, hd, ctx_out):
    rows = q.shape[0]
    blk = ATT_BLOCK
    nb = seq // blk
    ncb = ctx_len // blk
    lat_blocks = batch * nb
    nq_blocks = nb + (ncb if ctx_out else 0)

    def q_idx(b, n):
        return (jnp.where(n < nb, b * nb + n, lat_blocks + b * ncb + (n - nb)), 0)

    def kv_idx(off):
        return lambda b, n: (b * nb + jnp.clip(n + off, 0, nb - 1), 0)

    def ctx_idx(b, n):
        return (lat_blocks * blk // ctx_len + b, 0)

    def bias_idx(b, n):
        return (jnp.where(n >= nb, 3, jnp.where(n == 0, 0, jnp.where(n == nb - 1, 2, 1))), 0, 0)

    kv_w = n_kv * hd
    kv_specs = [pl.BlockSpec((blk, kv_w), kv_idx(-1)), pl.BlockSpec((blk, kv_w), kv_idx(0)),
                pl.BlockSpec((blk, kv_w), kv_idx(1)), pl.BlockSpec((ctx_len, kv_w), ctx_idx)]
    return pl.pallas_call(
        functools.partial(_attn_kernel, n_kv=n_kv, grp=n_q // n_kv, hd=hd),
        grid=(batch, nq_blocks),
        in_specs=[pl.BlockSpec(memory_space=pltpu.SMEM),
                  pl.BlockSpec((blk, n_q * hd), q_idx)] + kv_specs + kv_specs
                 + [pl.BlockSpec((None,) + bias.shape[1:], bias_idx)],
        out_specs=pl.BlockSpec((blk, n_q * hd), q_idx),
        out_shape=jax.ShapeDtypeStruct((rows if ctx_out else lat_blocks * blk, n_q * hd), BF16),
        compiler_params=_params("parallel", "parallel"),
        name="window_attention",
    )(sink, q, k, k, k, k, v, v, v, v, bias)


def _proj_res_kernel(a_ref, w_ref, x_ref, mod_ref, o_ref):
    a = a_ref[...]
    for n0 in range(0, o_ref.shape[1], MLP_OUT_CHUNK):
        cols = slice(n0, n0 + MLP_OUT_CHUNK)
        y = jnp.dot(a, w_ref[:, cols], preferred_element_type=F32)
        o_ref[:, cols] = x_ref[:, cols] + mod_ref[2:3, cols] * y


def _proj_residual(a, w, xs, mod, *, layer, rows, tm, mod_index):
    _, kdim, d = w.shape
    return pl.pallas_call(
        _proj_res_kernel,
        grid=(rows // tm,),
        in_specs=[
            pl.BlockSpec((tm, kdim), lambda t: (t, 0)),
            pl.BlockSpec((None, kdim, d), lambda t: (layer, 0, 0), pipeline_mode=pl.Buffered(1)),
            pl.BlockSpec((tm, d), lambda t: (t, 0)),
            pl.BlockSpec((None, N_MOD, d), lambda t: (mod_index(t), 0, 0)),
        ],
        out_specs=pl.BlockSpec((tm, d), lambda t: (t, 0)),
        out_shape=jax.ShapeDtypeStruct(xs.shape, F32),
        input_output_aliases={2: 0},
        compiler_params=_params("parallel"),
        name="proj_residual",
    )(a, w, xs, mod)


def _rope_tables(seq, tm, hd, scale):
    q = hd // 4
    pos = jnp.arange(seq)
    row = (pos // GRID_W).astype(F32)[:, None]
    col = (pos % GRID_W).astype(F32)[:, None]
    axis_dim = hd // 2
    inv_freq = ROPE_BASE ** (-jnp.arange(0, axis_dim, 2, dtype=F32) / axis_dim)
    a_row, a_col = row * inv_freq, col * inv_freq
    cos = jnp.concatenate([jnp.cos(a_row), jnp.cos(a_col)] * 2, axis=1)
    sin = jnp.concatenate([-jnp.sin(a_row), -jnp.sin(a_col), jnp.sin(a_row), jnp.sin(a_col)], axis=1)
    assert cos.shape == (seq, 4 * q)
    cos = jnp.concatenate([cos.reshape(seq // tm, tm, hd), jnp.ones((1, tm, hd), F32)], axis=0)
    sin = jnp.concatenate([sin.reshape(seq // tm, tm, hd), jnp.zeros((1, tm, hd), F32)], axis=0)
    return cos * scale, sin * scale, cos, sin


def _head_perm(n_heads, hd):
    q = hd // 4
    one = jnp.concatenate([jnp.arange(0, q), jnp.arange(2 * q, 3 * q), jnp.arange(q, 2 * q), jnp.arange(3 * q, 4 * q)])
    return (jnp.arange(n_heads)[:, None] * hd + one[None, :]).reshape(-1)


def _attn_bias(grp, ctx_len):
    blk = ATT_BLOCK
    qi = jnp.arange(grp * blk)[:, None] % blk
    kj = jnp.arange(3 * blk + ctx_len)[None, :]
    is_ctx = kj >= 3 * blk
    win = jnp.abs(kj - blk - qi) <= ATT_BLOCK
    variants = [win & (kj >= blk), win, win & (kj < 2 * blk), jnp.zeros_like(win)]
    return jnp.stack([jnp.where(is_ctx | m, 0.0, -jnp.inf).astype(F32) for m in variants])


def _gmlp_gate_kernel(x_hbm, mod_ref, g_ref, w_ref, gv_ref, ws_ref, bs_ref, z_ref,
                      x_buf, h_ref, v_ref, mu_ref, rs_ref, x_sem, *, n_vt, chunk, cg):
    t, j = pl.program_id(0), pl.program_id(1)
    nt = pl.num_programs(0)
    tm, tn = z_ref.shape
    width = n_vt * tn
    row_block = min(GMLP_ROW_BLOCK, tm)

    def x_copy(tile):
        return pltpu.make_async_copy(x_hbm.at[pl.ds(tile * tm, tm), :], x_buf, x_sem)

    def gelu_proj(h, cols):
        return jax.nn.gelu(jnp.dot(h, w_ref[:, cols], preferred_element_type=F32))

    def v_cols(h, rows, tile):
        tot = jnp.zeros((h.shape[0], 1), F32)
        for n0 in range(0, tn, GMLP_COL_CHUNK):
            cols = slice(n0, n0 + GMLP_COL_CHUNK)
            y = gelu_proj(h, cols)
            v_ref[tile, rows, cols] = y.astype(BF16)
            tot += jnp.sum(y, axis=-1, keepdims=True)
        return tot

    def for_row_blocks(n_rows, body):
        def step(i, carry):
            body(pl.multiple_of(i * n_rows, n_rows))
            return carry
        lax.fori_loop(0, tm // n_rows, step, 0)

    def v_stats(base):
        for r0 in range(0, row_block, ROW_CHUNK):
            rows = pl.ds(base + r0, ROW_CHUNK)
            mu = mu_ref[rows, :] / width
            mu_wide = jnp.concatenate([mu] * (tn // LANE), axis=1)
            sq = jnp.zeros((ROW_CHUNK, 1), F32)
            for k in range(n_vt):
                dv = v_ref[k, rows, :].astype(F32) - mu_wide
                sq += jnp.sum(dv * dv, axis=-1, keepdims=True)
            mu_ref[rows, :] = mu
            rs_ref[rows, :] = jnp.broadcast_to(lax.rsqrt(sq / width + EPS), (ROW_CHUNK, LANE))

    def u_rows(jj, base):
        rep = cg // LANE
        h = h_ref[pl.ds(base, row_block), :]
        for n0 in range(0, tn, GMLP_COL_CHUNK):
            u = gelu_proj(h, slice(n0, n0 + GMLP_COL_CHUNK))
            for g0 in range(0, GMLP_COL_CHUNK, cg):
                gi = jj * (tn // cg) + (n0 + g0) // cg
                cols = slice(n0 + g0, n0 + g0 + cg)
                w_s = ws_ref[gi]
                b_s = jnp.concatenate([bs_ref[gi]] * rep, axis=1)
                gv = gv_ref[jj, :, cols]
                for r0 in range(0, row_block, chunk):
                    rows = pl.ds(base + r0, chunk)
                    mu = jnp.concatenate([mu_ref[rows, :]] * rep, axis=1)
                    rs = jnp.concatenate([rs_ref[rows, :]] * rep, axis=1)
                    vn = ((v_ref[jj, rows, cols].astype(F32) - mu) * rs * gv).astype(BF16)
                    sv = jnp.dot(w_s, vn, preferred_element_type=F32) + b_s
                    z_ref[rows, cols] = (u[r0:r0 + chunk, g0:g0 + cg] * sv).astype(BF16)

    @pl.when(jnp.logical_and(t == 0, j == 0))
    def _fetch_first_tile():
        x_copy(0).start()

    @pl.when(j == 0)
    def _first_v_tile():
        x_copy(t).wait()

        def body(base):
            for r0 in range(0, row_block, MLP_ROW_CHUNK):
                rows = pl.ds(base + r0, MLP_ROW_CHUNK)
                h = _normmod(x_buf[rows, :], g_ref[...], mod_ref[0:1, :], mod_ref[1:2, :]).astype(BF16)
                h_ref[rows, :] = h
                mu_ref[rows, :] = jnp.broadcast_to(v_cols(h, rows, 0), (MLP_ROW_CHUNK, LANE))
        for_row_blocks(row_block, body)

    @pl.when(jnp.logical_and(j == 1, t + 1 < nt))
    def _prefetch_next_tile():
        x_copy(t + 1).start()

    @pl.when(jnp.logical_and(j > 0, j < n_vt))
    def _v_tile():
        def body(base):
            rows = pl.ds(base, row_block)
            mu_ref[rows, :] += jnp.broadcast_to(v_cols(h_ref[rows, :], rows, j), (row_block, LANE))
        for_row_blocks(row_block, body)

    @pl.when(j == n_vt)
    def _first_u_tile():
        def body(base):
            v_stats(base)
            u_rows(0, base)
        for_row_blocks(row_block, body)

    @pl.when(j > n_vt)
    def _u_tile():
        for_row_blocks(row_block, functools.partial(u_rows, j - n_vt))


def _gmlp_gate(xs, mod, g, w_in, g_v, w_s, b_s, *, layer, tm, mod_index):
    rows, d = xs.shape
    tn = GMLP_IN_TILE
    width = w_in.shape[2] // 2
    n_vt = width // tn
    groups, chunk = w_s.shape[1], w_s.shape[2]
    cg = width // groups
    assert tn % GMLP_COL_CHUNK == 0 and GMLP_COL_CHUNK % cg == 0 and cg % LANE == 0 and tm % chunk == 0
    assert n_vt >= 1 and min(GMLP_ROW_BLOCK, tm) % MLP_ROW_CHUNK == 0
    g_v_tiles = g_v[layer].reshape(n_vt, 1, tn)
    b_s_wide = jnp.broadcast_to(b_s[layer][:, :, None], (groups, chunk, LANE))
    return pl.pallas_call(
        functools.partial(_gmlp_gate_kernel, n_vt=n_vt, chunk=chunk, cg=cg),
        grid=(rows // tm, 2 * n_vt),
        in_specs=[
            pl.BlockSpec(memory_space=pl.ANY),
            pl.BlockSpec((None, N_MOD, d), lambda t, j: (mod_index(t), 0, 0)),
            pl.BlockSpec((1, d), lambda t, j: (0, 0)),
            pl.BlockSpec((None, d, tn), lambda t, j: (layer, 0, (j + n_vt) % (2 * n_vt))),
            pl.BlockSpec((n_vt, 1, tn), lambda t, j: (0, 0, 0)),
            pl.BlockSpec((None, groups, chunk, chunk), lambda t, j: (layer, 0, 0, 0)),
            pl.BlockSpec((groups, chunk, LANE), lambda t, j: (0, 0, 0)),
        ],
        out_specs=pl.BlockSpec((tm, tn), lambda t, j: (t, jnp.maximum(j - n_vt, 0))),
        out_shape=jax.ShapeDtypeStruct((rows, width), BF16),
        scratch_shapes=[pltpu.VMEM((tm, d), F32), pltpu.VMEM((tm, d), BF16),
                        pltpu.VMEM((n_vt, tm, tn), BF16),
                        pltpu.VMEM((tm, LANE), F32), pltpu.VMEM((tm, LANE), F32),
                        pltpu.SemaphoreType.DMA(())],
        compiler_params=_params("arbitrary", "arbitrary"),
        name="gmlp_gate",
    )(xs, mod, g, w_in, g_v_tiles, w_s, b_s_wide)


def _pool_kernel(x_ref, xp_ref, xn_ref, mod_ref, g_ref, w_ref, sc_ref, o_ref, e_ref,
                 *, tm, lat_tiles, seq, ctx_len):
    t = pl.program_id(0)
    is_lat = t < lat_tiles
    seq_len = jnp.where(is_lat, seq, ctx_len)
    tiles_per_seq = jnp.where(is_lat, seq // tm, ctx_len // tm)
    p = jnp.where(is_lat, t % (seq // tm), (t - lat_tiles) % (ctx_len // tm))
    has_prev = (p > 0).astype(F32)
    has_next = (p < tiles_per_seq - 1).astype(F32)

    g, sh, sc = g_ref[...], mod_ref[0:1, :], mod_ref[1:2, :]
    x = x_ref[...]
    halo = POOL_HALO
    e_ref[0:halo, :] = _normmod(xp_ref[...], g, sh, sc) * has_prev
    e_ref[halo:halo + tm, :] = _normmod(x, g, sh, sc)
    e_ref[halo + tm:2 * halo + tm, :] = _normmod(xn_ref[...], g, sh, sc) * has_next

    pos = p * tm + lax.broadcasted_iota(jnp.int32, (tm, 1), 0)
    n_ext = tm + 2 * halo
    cgw = x.shape[1] // len(POOL_SIZES)
    for gi, w in enumerate(POOL_SIZES):
        cols = slice(gi * cgw, (gi + 1) * cgw)
        e = e_ref[:, cols]
        a = e + pltpu.roll(e, 1, axis=0)
        step = 1
        while 2 * step < w:
            a = pltpu.roll(a, step, axis=0) + pltpu.roll(a, n_ext - step, axis=0)
            step *= 2
        cnt = (jnp.minimum(pos + w // 2, seq_len) - jnp.maximum(pos - w // 2, 0)).astype(F32)
        dlt = a[halo:halo + tm, :] / cnt - e[halo:halo + tm, :]
        y = jnp.dot(dlt.astype(BF16), w_ref[gi], preferred_element_type=F32) * sc_ref[:, cols]
        o_ref[:, cols] = x[:, cols] + mod_ref[2:3, cols] * y


def _pool_residual(xs, mod, g, w_pool, scale, *, tm, lat_rows, seq, ctx_len, mod_index):
    rows, d = xs.shape
    hb = tm // POOL_HALO
    n_hblocks = rows // POOL_HALO
    return pl.pallas_call(
        functools.partial(_pool_kernel, tm=tm, lat_tiles=lat_rows // tm, seq=seq, ctx_len=ctx_len),
        grid=(rows // tm,),
        in_specs=[
            pl.BlockSpec((tm, d), lambda t: (t, 0)),
            pl.BlockSpec((POOL_HALO, d), lambda t: (jnp.maximum(t * hb - 1, 0), 0)),
            pl.BlockSpec((POOL_HALO, d), lambda t: (jnp.minimum((t + 1) * hb, n_hblocks - 1), 0)),
            pl.BlockSpec((None, N_MOD, d), lambda t: (mod_index(t), 0, 0)),
            pl.BlockSpec((1, d), lambda t: (0, 0)),
            pl.BlockSpec(w_pool.shape, lambda t: (0, 0, 0)),
            pl.BlockSpec((1, d), lambda t: (0, 0)),
        ],
        out_specs=pl.BlockSpec((tm, d), lambda t: (t, 0)),
        out_shape=jax.ShapeDtypeStruct(xs.shape, F32),
        scratch_shapes=[pltpu.VMEM((tm + 2 * POOL_HALO, d), F32)],
        compiler_params=_params("parallel"),
        name="pool_mixer",
    )(xs, xs, xs, mod, g, w_pool, scale)


def kernel(x, c, ctx, c_ctx, w_mod, b_mod, g_norm, w_up, w_down, attn_wq, attn_wk, attn_wv, attn_wo, attn_sink, gmlp_w_in, gmlp_g_v, gmlp_w_s, gmlp_b_s, gmlp_w_out, pool_w, pool_scale, g_final):
    B, S, D = x.shape
    L = ctx.shape[1]
    depth = w_mod.shape[0]
    n_q = attn_sink.shape[1]
    hd = attn_wq.shape[2] // n_q
    n_kv = attn_wk.shape[2] // hd
    lat_rows, ctx_rows = B * S, B * L

    tm = min(1024, S, ctx_rows)
    tm_qkv = min(512, S, ctx_rows)
    tm_proj = min(512, S, ctx_rows)
    tm_pool = min(256, S, L)
    assert S % tm == 0 and ctx_rows % tm == 0 and S % ATT_BLOCK == 0 and L % ATT_BLOCK == 0
    assert lat_rows % L == 0 and S // ATT_BLOCK >= 2 and S % GRID_W == 0

    def mod_index_for(tile):
        return lambda t: jnp.minimum(t // (S // tile), B)

    assert N_MIXERS == 3 and depth >= 1
    xs = None

    n_cond = -(-(B + 1) // SUBLANE) * SUBLANE
    cond = jnp.concatenate([c, c_ctx[None, :], jnp.zeros((n_cond - B - 1, D), F32)], axis=0)
    mods = _modulation(cond, w_mod, b_mod).reshape(depth, n_cond, N_MOD, D)

    tables = _rope_tables(S, tm_qkv, hd, hd ** -0.5 * LOG2_E)
    bias = _attn_bias(n_q // n_kv, L)
    qperm, kperm = _head_perm(n_q, hd), _head_perm(n_kv, hd)

    def table_index(t):
        return jnp.where(t < lat_rows // tm_qkv, t % (S // tm_qkv), S // tm_qkv)

    w_up_bf, w_down_bf = w_up[0].astype(BF16), w_down[0].astype(BF16)
    attn_wo_bf = attn_wo.astype(BF16)
    gmlp_w_in_bf, gmlp_w_s_bf, gmlp_w_out_bf = gmlp_w_in.astype(BF16), gmlp_w_s.astype(BF16), gmlp_w_out.astype(BF16)

    out = None
    for i in range(depth):
        kind, j = i % N_MIXERS, i // N_MIXERS
        last = i == depth - 1
        mod = mods[i]
        g_a, g_m = g_norm[i, 0][None, :], g_norm[i, 1][None, :]
        res_rows = lat_rows if last else lat_rows + ctx_rows

        if kind == 0:
            wqkv = jnp.concatenate([attn_wq[j][:, qperm], attn_wk[j][:, kperm], attn_wv[j]], axis=1).astype(BF16)
            streams = (x.reshape(lat_rows, D), ctx.reshape(ctx_rows, D)) if i == 0 else (xs,)
            qkv_out = _qkv(streams, mod, g_a, wqkv, tables, tm=tm_qkv, n_q=n_q, n_kv=n_kv, hd=hd,
                           mod_index=mod_index_for(tm_qkv), table_index=table_index)
            q, k, v = qkv_out[:3]
            if i == 0:
                xs = qkv_out[3]
            o = _attention(q, k, v, attn_sink[j] * LOG2_E, bias, batch=B, seq=S, ctx_len=L,
                           n_q=n_q, n_kv=n_kv, hd=hd, ctx_out=not last)
            xs = _proj_residual(o, attn_wo_bf, xs, mod, layer=j, rows=res_rows, tm=tm_proj,
                                mod_index=mod_index_for(tm_proj))
        elif kind == 1:
            z = _gmlp_gate(xs, mod, g_a, gmlp_w_in_bf, gmlp_g_v, gmlp_w_s_bf, gmlp_b_s,
                           layer=j, tm=tm, mod_index=mod_index_for(tm))
            xs = _proj_residual(z, gmlp_w_out_bf, xs, mod, layer=j, rows=res_rows, tm=tm_proj,
                                mod_index=mod_index_for(tm_proj))
        else:
            xs = _pool_residual(xs, mod, g_a, pool_w[j].astype(BF16), pool_scale[j][None, :],
                                tm=tm_pool, lat_rows=lat_rows, seq=S, ctx_len=L,
                                mod_index=mod_index_for(tm_pool))

        res = _mlp(xs, mod, g_m, w_up_bf, w_down_bf, g_final[None, :], rows=res_rows, tm=tm,
                   mod_index=mod_index_for(tm), final=last,
                   next_weights=None if last else (w_up, w_down, i + 1))
        if last:
            out = res[0]
        else:
            xs, w_up_bf, w_down_bf = res
    return out.reshape(B, S, D)
```

```python
import functools

import jax
import jax.numpy as jnp
from jax import lax
from jax.experimental import pallas as pl
from jax.experimental.pallas import tpu as pltpu

EPS = 1e-6
N_MOD = 6
N_MIXERS = 3
GRID_W = 64
ATT_BLOCK = 128
ROPE_BASE = 10000.0
LOG2_E = 1.4426950408889634
POOL_SIZES = (2, 4, 8, 16)
POOL_HALO = 8
LANE = 128
SUBLANE = 8
VMEM_LIMIT_BYTES = 56 * 1024 * 1024

MOD_COL_TILE = 1024
MLP_FF_TILE = 1024
MLP_OUT_CHUNK = 512
NORM_ROW_CHUNK = 256
MLP_CAST_STEPS = 128
QKV_HEADS_PER_DOT = 4
GMLP_IN_TILE = 2048
GMLP_COL_CHUNK = 256
GMLP_ROW_BLOCK = 512
STAT_ROW_CHUNK = 128

BF16 = jnp.bfloat16
F32 = jnp.float32


def _params(*sem):
    return pltpu.CompilerParams(dimension_semantics=sem, vmem_limit_bytes=VMEM_LIMIT_BYTES)


def _normmod(x, g, shift, scale):
    ms = jnp.mean(x * x, axis=-1, keepdims=True)
    return (x * lax.rsqrt(ms + EPS)) * g * (1.0 + scale) + shift


def _mod_kernel(c_ref, w_ref, b_ref, o_ref):
    c = c_ref[...]
    s = (c * jax.nn.sigmoid(c)).astype(BF16)
    o_ref[...] = jnp.dot(s, w_ref[...].astype(BF16), preferred_element_type=F32) + b_ref[...]


def _modulation(cond, w_mod, b_mod):
    depth, d, n = w_mod.shape
    r = cond.shape[0]
    tn = MOD_COL_TILE
    return pl.pallas_call(
        _mod_kernel,
        grid=(depth, n // tn),
        in_specs=[
            pl.BlockSpec((r, d), lambda i, j: (0, 0)),
            pl.BlockSpec((None, d, tn), lambda i, j: (i, 0, j)),
            pl.BlockSpec((None, 1, tn), lambda i, j: (i, 0, j)),
        ],
        out_specs=pl.BlockSpec((None, r, tn), lambda i, j: (i, 0, j)),
        out_shape=jax.ShapeDtypeStruct((depth, r, n), F32),
        compiler_params=_params("parallel", "parallel"),
        name="modulation",
    )(cond, w_mod, b_mod.reshape(depth, 1, n))


def _mlp_kernel(*refs, final, n_cast):
    if n_cast:
        (x_hbm, mod_ref, g_ref, wu_ref, wd_ref, gf_ref, wun_ref, wdn_ref,
         o_ref, wun_o, wdn_o, x_buf, h_ref, x_sem) = refs
    else:
        x_hbm, mod_ref, g_ref, wu_ref, wd_ref, gf_ref, o_ref, x_buf, h_ref, x_sem = refs
    t, f = pl.program_id(0), pl.program_id(1)
    nt, nf = pl.num_programs(0), pl.num_programs(1)
    tm, d = o_ref.shape

    if n_cast:
        @pl.when(t * nf + f < n_cast)
        def _cast_next_layer_weights():
            wun_o[...] = wun_ref[...].astype(BF16)
            wdn_o[...] = wdn_ref[...].astype(BF16)

    def x_copy(tile):
        return pltpu.make_async_copy(x_hbm.at[pl.ds(tile * tm, tm), :], x_buf, x_sem)

    def up(h):
        u = jnp.dot(h, wu_ref[...], preferred_element_type=F32)
        return jnp.square(jnp.maximum(u, 0.0)).astype(BF16)

    def gate(cols):
        return mod_ref[5:6, cols]

    @pl.when(jnp.logical_and(t == 0, f == 0))
    def _fetch_first_tile():
        x_copy(0).start()

    @pl.when(f == 0)
    def _first():
        x_copy(t).wait()
        for r0 in range(0, tm, NORM_ROW_CHUNK):
            rows = slice(r0, r0 + NORM_ROW_CHUNK)
            h = _normmod(x_buf[rows, :], g_ref[...], mod_ref[3:4, :], mod_ref[4:5, :]).astype(BF16)
            h_ref[rows, :] = h
            u = up(h)
            for n0 in range(0, d, MLP_OUT_CHUNK):
                cols = slice(n0, n0 + MLP_OUT_CHUNK)
                y = jnp.dot(u, wd_ref[:, cols], preferred_element_type=F32)
                o_ref[rows, cols] = x_buf[rows, cols] + gate(cols) * y

    @pl.when(jnp.logical_and(f == 1, t + 1 < nt))
    def _prefetch_next_tile():
        x_copy(t + 1).start()

    @pl.when(jnp.logical_and(f > 0, f < nf - 1))
    def _middle():
        u = up(h_ref[...])
        for n0 in range(0, d, MLP_OUT_CHUNK):
            cols = slice(n0, n0 + MLP_OUT_CHUNK)
            o_ref[:, cols] += gate(cols) * jnp.dot(u, wd_ref[:, cols], preferred_element_type=F32)

    @pl.when(f == nf - 1)
    def _last():
        for r0 in range(0, tm, NORM_ROW_CHUNK):
            rows = slice(r0, r0 + NORM_ROW_CHUNK)
            u = up(h_ref[rows, :])
            y = o_ref[rows, :] + mod_ref[5:6, :] * jnp.dot(u, wd_ref[...], preferred_element_type=F32)
            if final:
                ms = jnp.mean(y * y, axis=-1, keepdims=True)
                y = (y * lax.rsqrt(ms + EPS)) * gf_ref[...]
            o_ref[rows, :] = y


def _mlp(xs, mod, g, w_up, w_down, g_final, *, rows, tm, mod_index, final, next_weights=None):
    d, ff = w_up.shape
    tf = MLP_FF_TILE
    nt, nf = rows // tm, ff // tf
    assert nf >= 3 and tm % NORM_ROW_CHUNK == 0
    in_specs = [
        pl.BlockSpec(memory_space=pl.ANY),
        pl.BlockSpec((None, N_MOD, d), lambda t, f: (mod_index(t), 0, 0)),
        pl.BlockSpec((1, d), lambda t, f: (0, 0)),
        pl.BlockSpec((d, tf), lambda t, f: (0, f)),
        pl.BlockSpec((tf, d), lambda t, f: (f, 0)),
        pl.BlockSpec((1, d), lambda t, f: (0, 0)),
    ]
    out_specs = [pl.BlockSpec((tm, d), lambda t, f: (t, 0))]
    out_shape = [jax.ShapeDtypeStruct((rows, d), F32)]
    operands = [xs, mod, g, w_up, w_down, g_final]
    n_cast = 0
    if next_weights is not None:
        wu32, wd32, nxt = next_weights
        n_cast = MLP_CAST_STEPS
        while n_cast > nt * nf:
            n_cast //= 2
        cu, cd = d // n_cast, ff // n_cast
        assert d % n_cast == 0 and cu % (2 * SUBLANE) == 0

        def slab(t, f):
            return jnp.minimum(t * nf + f, n_cast - 1)
        in_specs += [pl.BlockSpec((None, cu, ff), lambda t, f: (nxt, slab(t, f), 0)),
                     pl.BlockSpec((None, cd, d), lambda t, f: (nxt, slab(t, f), 0))]
        out_specs += [pl.BlockSpec((cu, ff), lambda t, f: (slab(t, f), 0)),
                      pl.BlockSpec((cd, d), lambda t, f: (slab(t, f), 0))]
        out_shape += [jax.ShapeDtypeStruct((d, ff), BF16), jax.ShapeDtypeStruct((ff, d), BF16)]
        operands += [wu32, wd32]
    return pl.pallas_call(
        functools.partial(_mlp_kernel, final=final, n_cast=n_cast),
        grid=(nt, nf),
        in_specs=in_specs,
        out_specs=out_specs,
        out_shape=out_shape,
        scratch_shapes=[pltpu.VMEM((tm, d), F32), pltpu.VMEM((tm, d), BF16), pltpu.SemaphoreType.DMA(())],
        compiler_params=_params("arbitrary", "arbitrary"),
        name="mlp_final" if final else "mlp",
    )(*operands)


def _qkv_kernel(*refs, n_q, n_kv, hd, lat_tiles):
    if lat_tiles is None:
        x_ref, mod_ref, g_ref, w_ref, cq_ref, sq_ref, ck_ref, sk_ref, q_ref, k_ref, v_ref = refs
    else:
        (xl_ref, xc_ref, mod_ref, g_ref, w_ref, cq_ref, sq_ref, ck_ref, sk_ref,
         q_ref, k_ref, v_ref, xs_ref) = refs
        is_lat = pl.program_id(0) < lat_tiles
    group = QKV_HEADS_PER_DOT
    for r0 in range(0, q_ref.shape[0], NORM_ROW_CHUNK):
        rows = slice(r0, r0 + NORM_ROW_CHUNK)
        if lat_tiles is None:
            x = x_ref[rows, :]
        else:
            x = jnp.where(is_lat, xl_ref[rows, :], xc_ref[rows, :])
            xs_ref[rows, :] = x
        h = _normmod(x, g_ref[...], mod_ref[0:1, :], mod_ref[1:2, :]).astype(BF16)

        def rope(y, c_ref, s_ref):
            return y * c_ref[rows, :] + pltpu.roll(y, hd // 2, axis=1) * s_ref[rows, :]

        for h0 in range(0, n_q, group):
            acc = jnp.dot(h, w_ref[:, h0 * hd:(h0 + group) * hd], preferred_element_type=F32)
            for j in range(group):
                y = acc[:, j * hd:(j + 1) * hd]
                q_ref[rows, (h0 + j) * hd:(h0 + j + 1) * hd] = rope(y, cq_ref, sq_ref).astype(BF16)
        acc = jnp.dot(h, w_ref[:, n_q * hd:(n_q + n_kv) * hd], preferred_element_type=F32)
        for j in range(n_kv):
            k_ref[rows, j * hd:(j + 1) * hd] = rope(acc[:, j * hd:(j + 1) * hd], ck_ref, sk_ref).astype(BF16)
        acc = jnp.dot(h, w_ref[:, (n_q + n_kv) * hd:(n_q + 2 * n_kv) * hd], preferred_element_type=F32)
        v_ref[rows, :] = acc.astype(BF16)


def _qkv(streams, mod, g, wqkv, tables, *, tm, n_q, n_kv, hd, mod_index, table_index):
    split = len(streams) == 2
    rows = sum(s.shape[0] for s in streams)
    d = streams[0].shape[1]
    lat_tiles = streams[0].shape[0] // tm if split else None
    tab_spec = pl.BlockSpec((None, tm, hd), lambda t: (table_index(t), 0, 0))
    if split:
        x_specs = [pl.BlockSpec((tm, d), lambda t: (jnp.minimum(t, lat_tiles - 1), 0)),
                   pl.BlockSpec((tm, d), lambda t: (jnp.maximum(t - lat_tiles, 0), 0))]
    else:
        x_specs = [pl.BlockSpec((tm, d), lambda t: (t, 0))]
    out_specs = [
        pl.BlockSpec((tm, n_q * hd), lambda t: (t, 0)),
        pl.BlockSpec((tm, n_kv * hd), lambda t: (t, 0)),
        pl.BlockSpec((tm, n_kv * hd), lambda t: (t, 0)),
    ]
    out_shape = [
        jax.ShapeDtypeStruct((rows, n_q * hd), BF16),
        jax.ShapeDtypeStruct((rows, n_kv * hd), BF16),
        jax.ShapeDtypeStruct((rows, n_kv * hd), BF16),
    ]
    if split:
        out_specs.append(pl.BlockSpec((tm, d), lambda t: (t, 0)))
        out_shape.append(jax.ShapeDtypeStruct((rows, d), F32))
    return pl.pallas_call(
        functools.partial(_qkv_kernel, n_q=n_q, n_kv=n_kv, hd=hd, lat_tiles=lat_tiles),
        grid=(rows // tm,),
        in_specs=x_specs + [
            pl.BlockSpec((None, N_MOD, d), lambda t: (mod_index(t), 0, 0)),
            pl.BlockSpec((1, d), lambda t: (0, 0)),
            pl.BlockSpec(wqkv.shape, lambda t: (0, 0), pipeline_mode=pl.Buffered(1)),
            tab_spec, tab_spec, tab_spec, tab_spec,
        ],
        out_specs=out_specs,
        out_shape=out_shape,
        compiler_params=_params("parallel"),
        name="qkv_rope_join" if split else "qkv_rope",
    )(*streams, mod, g, wqkv, *tables)


def _attn_kernel(sink_ref, q_ref, kp_ref, kc_ref, kn_ref, kx_ref, vp_ref, vc_ref, vn_ref, vx_ref,
                 bias_ref, o_ref, *, n_kv, grp, hd):
    bias = bias_ref[...]
    blk = q_ref.shape[0]
    n_keys = bias.shape[1]
    ones = jnp.ones((n_keys, hd), BF16)
    for hk in range(n_kv):
        sl = slice(hk * hd, (hk + 1) * hd)
        k_all = jnp.concatenate([kp_ref[:, sl], kc_ref[:, sl], kn_ref[:, sl], kx_ref[:, sl]], axis=0)
        v_all = jnp.concatenate([vp_ref[:, sl], vc_ref[:, sl], vn_ref[:, sl], vx_ref[:, sl]], axis=0)
        v_aug = jnp.concatenate([v_all, ones], axis=1)
        heads = [hk * grp + j for j in range(grp)]
        qh = jnp.concatenate([q_ref[:, hq * hd:(hq + 1) * hd] for hq in heads], axis=0)
        s = lax.dot_general(qh, k_all, (((1,), (1,)), ((), ())), preferred_element_type=F32) + bias
        sink = jnp.concatenate([jnp.full((blk, 1), sink_ref[hq], F32) for hq in heads], axis=0)
        m = jnp.maximum(jnp.max(s, axis=-1, keepdims=True), sink)
        p = jnp.exp2(s - m).astype(BF16)
        ov = jnp.dot(p, v_aug, preferred_element_type=F32)
        o = ov[:, :hd] / (ov[:, hd:] + jnp.exp2(sink - m))
        for j, hq in enumerate(heads):
            o_ref[:, hq * hd:(hq + 1) * hd] = o[j * blk:(j + 1) * blk, :].astype(o_ref.dtype)


def _attention(q, k, v, sink, bias, *, batch, seq, ctx_len, n_q, n_kv, hd, ctx_out):
    rows = q.shape[0]
    blk = ATT_BLOCK
    nb = seq // blk
    ncb = ctx_len // blk
    lat_blocks = batch * nb
    nq_blocks = nb + (ncb if ctx_out else 0)

    def q_idx(b, n):
        return (jnp.where(n < nb, b * nb + n, lat_blocks + b * ncb + (n - nb)), 0)

    def kv_idx(off):
        return lambda b, n: (b * nb + jnp.clip(n + off, 0, nb - 1), 0)

    def ctx_idx(b, n):
        return (lat_blocks * blk // ctx_len + b, 0)

    def bias_idx(b, n):
        return (jnp.where(n >= nb, 3, jnp.where(n == 0, 0, jnp.where(n == nb - 1, 2, 1))), 0, 0)

    kv_w = n_kv * hd
    kv_specs = [pl.BlockSpec((blk, kv_w), kv_idx(-1)), pl.BlockSpec((blk, kv_w), kv_idx(0)),
                pl.BlockSpec((blk, kv_w), kv_idx(1)), pl.BlockSpec((ctx_len, kv_w), ctx_idx)]
    return pl.pallas_call(
        functools.partial(_attn_kernel, n_kv=n_kv, grp=n_q // n_kv, hd=hd),
        grid=(batch, nq_blocks),
        in_specs=[pl.BlockSpec(memory_space=pltpu.SMEM),
                  pl.BlockSpec((blk, n_q * hd), q_idx)] + kv_specs + kv_specs
                 + [pl.BlockSpec((None,) + bias.shape[1:], bias_idx)],
        out_specs=pl.BlockSpec((blk, n_q * hd), q_idx),
        out_shape=jax.ShapeDtypeStruct((rows if ctx_out else lat_blocks * blk, n_q * hd), BF16),
        compiler_params=_params("parallel", "parallel"),
        name="window_attention",
    )(sink, q, k, k, k, k, v, v, v, v, bias)


def _proj_res_kernel(a_ref, w_ref, x_ref, mod_ref, o_ref):
    a = a_ref[...]
    for n0 in range(0, o_ref.shape[1], MLP_OUT_CHUNK):
        cols = slice(n0, n0 + MLP_OUT_CHUNK)
        y = jnp.dot(a, w_ref[:, cols], preferred_element_type=F32)
        o_ref[:, cols] = x_ref[:, cols] + mod_ref[2:3, cols] * y


def _proj_residual(a, w, xs, mod, *, layer, rows, tm, mod_index):
    _, kdim, d = w.shape
    return pl.pallas_call(
        _proj_res_kernel,
        grid=(rows // tm,),
        in_specs=[
            pl.BlockSpec((tm, kdim), lambda t: (t, 0)),
            pl.BlockSpec((None, kdim, d), lambda t: (layer, 0, 0), pipeline_mode=pl.Buffered(1)),
            pl.BlockSpec((tm, d), lambda t: (t, 0)),
            pl.BlockSpec((None, N_MOD, d), lambda t: (mod_index(t), 0, 0)),
        ],
        out_specs=pl.BlockSpec((tm, d), lambda t: (t, 0)),
        out_shape=jax.ShapeDtypeStruct(xs.shape, F32),
        input_output_aliases={2: 0},
        compiler_params=_params("parallel"),
        name="proj_residual",
    )(a, w, xs, mod)


def _rope_tables(seq, tm, hd, scale):
    q = hd // 4
    pos = jnp.arange(seq)
    row = (pos // GRID_W).astype(F32)[:, None]
    col = (pos % GRID_W).astype(F32)[:, None]
    axis_dim = hd // 2
    inv_freq = ROPE_BASE ** (-jnp.arange(0, axis_dim, 2, dtype=F32) / axis_dim)
    a_row, a_col = row * inv_freq, col * inv_freq
    cos = jnp.concatenate([jnp.cos(a_row), jnp.cos(a_col)] * 2, axis=1)
    sin = jnp.concatenate([-jnp.sin(a_row), -jnp.sin(a_col), jnp.sin(a_row), jnp.sin(a_col)], axis=1)
    assert cos.shape == (seq, 4 * q)
    cos = jnp.concatenate([cos.reshape(seq // tm, tm, hd), jnp.ones((1, tm, hd), F32)], axis=0)
    sin = jnp.concatenate([sin.reshape(seq // tm, tm, hd), jnp.zeros((1, tm, hd), F32)], axis=0)
    return cos * scale, sin * scale, cos, sin


def _head_perm(n_heads, hd):
    q = hd // 4
    one = jnp.concatenate([jnp.arange(0, q), jnp.arange(2 * q, 3 * q), jnp.arange(q, 2 * q), jnp.arange(3 * q, 4 * q)])
    return (jnp.arange(n_heads)[:, None] * hd + one[None, :]).reshape(-1)


def _attn_bias(grp, ctx_len):
    blk = ATT_BLOCK
    qi = jnp.arange(grp * blk)[:, None] % blk
    kj = jnp.arange(3 * blk + ctx_len)[None, :]
    is_ctx = kj >= 3 * blk
    win = jnp.abs(kj - blk - qi) <= ATT_BLOCK
    variants = [win & (kj >= blk), win, win & (kj < 2 * blk), jnp.zeros_like(win)]
    return jnp.stack([jnp.where(is_ctx | m, 0.0, -jnp.inf).astype(F32) for m in variants])


def _gmlp_gate_kernel(x_hbm, mod_ref, g_ref, w_ref, gv_ref, ws_ref, bs_ref, z_ref,
                      x_buf, h_ref, v_ref, mu_ref, rs_ref, x_sem, *, n_vt, chunk, cg):
    t, j = pl.program_id(0), pl.program_id(1)
    nt = pl.num_programs(0)
    tm, tn = z_ref.shape
    width = n_vt * tn
    row_block = min(GMLP_ROW_BLOCK, tm)

    def x_copy(tile):
        return pltpu.make_async_copy(x_hbm.at[pl.ds(tile * tm, tm), :], x_buf, x_sem)

    def gelu_proj(h, cols):
        return jax.nn.gelu(jnp.dot(h, w_ref[:, cols], preferred_element_type=F32))

    def v_cols(h, rows, tile):
        tot = jnp.zeros((h.shape[0], 1), F32)
        for n0 in range(0, tn, GMLP_COL_CHUNK):
            cols = slice(n0, n0 + GMLP_COL_CHUNK)
            y = gelu_proj(h, cols)
            v_ref[tile, rows, cols] = y.astype(BF16)
            tot += jnp.sum(y, axis=-1, keepdims=True)
        return tot

    def for_row_blocks(n_rows, body):
        def step(i, carry):
            body(pl.multiple_of(i * n_rows, n_rows))
            return carry
        lax.fori_loop(0, tm // n_rows, step, 0)

    def v_stats(base):
        for r0 in range(0, row_block, STAT_ROW_CHUNK):
            rows = pl.ds(base + r0, STAT_ROW_CHUNK)
            mu = mu_ref[rows, :] / width
            mu_wide = jnp.concatenate([mu] * (tn // LANE), axis=1)
            sq = jnp.zeros((STAT_ROW_CHUNK, 1), F32)
            for k in range(n_vt):
                dv = v_ref[k, rows, :].astype(F32) - mu_wide
                sq += jnp.sum(dv * dv, axis=-1, keepdims=True)
            mu_ref[rows, :] = mu
            rs_ref[rows, :] = jnp.broadcast_to(lax.rsqrt(sq / width + EPS), (STAT_ROW_CHUNK, LANE))

    def u_rows(jj, base):
        rep = cg // LANE
        h = h_ref[pl.ds(base, row_block), :]
        for n0 in range(0, tn, GMLP_COL_CHUNK):
            u = gelu_proj(h, slice(n0, n0 + GMLP_COL_CHUNK))
            for g0 in range(0, GMLP_COL_CHUNK, cg):
                gi = jj * (tn // cg) + (n0 + g0) // cg
                cols = slice(n0 + g0, n0 + g0 + cg)
                w_s = ws_ref[gi]
                b_s = jnp.concatenate([bs_ref[gi]] * rep, axis=1)
                gv = gv_ref[jj, :, cols]
                for r0 in range(0, row_block, chunk):
                    rows = pl.ds(base + r0, chunk)
                    mu = jnp.concatenate([mu_ref[rows, :]] * rep, axis=1)
                    rs = jnp.concatenate([rs_ref[rows, :]] * rep, axis=1)
                    vn = ((v_ref[jj, rows, cols].astype(F32) - mu) * rs * gv).astype(BF16)
                    sv = jnp.dot(w_s, vn, preferred_element_type=F32) + b_s
                    z_ref[rows, cols] = (u[r0:r0 + chunk, g0:g0 + cg] * sv).astype(BF16)

    @pl.when(jnp.logical_and(t == 0, j == 0))
    def _fetch_first_tile():
        x_copy(0).start()

    @pl.when(j == 0)
    def _first_v_tile():
        x_copy(t).wait()

        def body(base):
            for r0 in range(0, row_block, NORM_ROW_CHUNK):
                rows = pl.ds(base + r0, NORM_ROW_CHUNK)
                h = _normmod(x_buf[rows, :], g_ref[...], mod_ref[0:1, :], mod_ref[1:2, :]).astype(BF16)
                h_ref[rows, :] = h
                mu_ref[rows, :] = jnp.broadcast_to(v_cols(h, rows, 0), (NORM_ROW_CHUNK, LANE))
        for_row_blocks(row_block, body)

    @pl.when(jnp.logical_and(j == 1, t + 1 < nt))
    def _prefetch_next_tile():
        x_copy(t + 1).start()

    @pl.when(jnp.logical_and(j > 0, j < n_vt))
    def _v_tile():
        def body(base):
            rows = pl.ds(base, row_block)
            mu_ref[rows, :] += jnp.broadcast_to(v_cols(h_ref[rows, :], rows, j), (row_block, LANE))
        for_row_blocks(row_block, body)

    @pl.when(j == n_vt)
    def _first_u_tile():
        def body(base):
            v_stats(base)
            u_rows(0, base)
        for_row_blocks(row_block, body)

    @pl.when(j > n_vt)
    def _u_tile():
        for_row_blocks(row_block, functools.partial(u_rows, j - n_vt))


def _gmlp_gate(xs, mod, g, w_in, g_v, w_s, b_s, *, layer, tm, mod_index):
    rows, d = xs.shape
    tn = GMLP_IN_TILE
    width = w_in.shape[2] // 2
    n_vt = width // tn
    groups, chunk = w_s.shape[1], w_s.shape[2]
    cg = width // groups
    assert tn % GMLP_COL_CHUNK == 0 and GMLP_COL_CHUNK % cg == 0 and cg % LANE == 0 and tm % chunk == 0
    assert n_vt >= 1 and min(GMLP_ROW_BLOCK, tm) % NORM_ROW_CHUNK == 0
    g_v_tiles = g_v[layer].reshape(n_vt, 1, tn)
    b_s_wide = jnp.broadcast_to(b_s[layer][:, :, None], (groups, chunk, LANE))
    return pl.pallas_call(
        functools.partial(_gmlp_gate_kernel, n_vt=n_vt, chunk=chunk, cg=cg),
        grid=(rows // tm, 2 * n_vt),
        in_specs=[
            pl.BlockSpec(memory_space=pl.ANY),
            pl.BlockSpec((None, N_MOD, d), lambda t, j: (mod_index(t), 0, 0)),
            pl.BlockSpec((1, d), lambda t, j: (0, 0)),
            pl.BlockSpec((None, d, tn), lambda t, j: (layer, 0, (j + n_vt) % (2 * n_vt))),
            pl.BlockSpec((n_vt, 1, tn), lambda t, j: (0, 0, 0)),
            pl.BlockSpec((None, groups, chunk, chunk), lambda t, j: (layer, 0, 0, 0)),
            pl.BlockSpec((groups, chunk, LANE), lambda t, j: (0, 0, 0)),
        ],
        out_specs=pl.BlockSpec((tm, tn), lambda t, j: (t, jnp.maximum(j - n_vt, 0))),
        out_shape=jax.ShapeDtypeStruct((rows, width), BF16),
        scratch_shapes=[pltpu.VMEM((tm, d), F32), pltpu.VMEM((tm, d), BF16),
                        pltpu.VMEM((n_vt, tm, tn), BF16),
                        pltpu.VMEM((tm, LANE), F32), pltpu.VMEM((tm, LANE), F32),
                        pltpu.SemaphoreType.DMA(())],
        compiler_params=_params("arbitrary", "arbitrary"),
        name="gmlp_gate",
    )(xs, mod, g, w_in, g_v_tiles, w_s, b_s_wide)


def _pool_kernel(x_ref, xp_ref, xn_ref, mod_ref, g_ref, w_ref, sc_ref, o_ref, e_ref,
                 *, tm, lat_tiles, seq, ctx_len):
    t = pl.program_id(0)
    is_lat = t < lat_tiles
    seq_len = jnp.where(is_lat, seq, ctx_len)
    tiles_per_seq = jnp.where(is_lat, seq // tm, ctx_len // tm)
    p = jnp.where(is_lat, t % (seq // tm), (t - lat_tiles) % (ctx_len // tm))
    has_prev = (p > 0).astype(F32)
    has_next = (p < tiles_per_seq - 1).astype(F32)

    g, sh, sc = g_ref[...], mod_ref[0:1, :], mod_ref[1:2, :]
    x = x_ref[...]
    halo = POOL_HALO
    e_ref[0:halo, :] = _normmod(xp_ref[...], g, sh, sc) * has_prev
    e_ref[halo:halo + tm, :] = _normmod(x, g, sh, sc)
    e_ref[halo + tm:2 * halo + tm, :] = _normmod(xn_ref[...], g, sh, sc) * has_next

    pos = p * tm + lax.broadcasted_iota(jnp.int32, (tm, 1), 0)
    n_ext = tm + 2 * halo
    cgw = x.shape[1] // len(POOL_SIZES)
    for gi, w in enumerate(POOL_SIZES):
        cols = slice(gi * cgw, (gi + 1) * cgw)
        e = e_ref[:, cols]
        a = e + pltpu.roll(e, 1, axis=0)
        step = 1
        while 2 * step < w:
            a = pltpu.roll(a, step, axis=0) + pltpu.roll(a, n_ext - step, axis=0)
            step *= 2
        cnt = (jnp.minimum(pos + w // 2, seq_len) - jnp.maximum(pos - w // 2, 0)).astype(F32)
        dlt = a[halo:halo + tm, :] / cnt - e[halo:halo + tm, :]
        y = jnp.dot(dlt.astype(BF16), w_ref[gi], preferred_element_type=F32) * sc_ref[:, cols]
        o_ref[:, cols] = x[:, cols] + mod_ref[2:3, cols] * y


def _pool_residual(xs, mod, g, w_pool, scale, *, tm, lat_rows, seq, ctx_len, mod_index):
    rows, d = xs.shape
    hb = tm // POOL_HALO
    n_hblocks = rows // POOL_HALO
    return pl.pallas_call(
        functools.partial(_pool_kernel, tm=tm, lat_tiles=lat_rows // tm, seq=seq, ctx_len=ctx_len),
        grid=(rows // tm,),
        in_specs=[
            pl.BlockSpec((tm, d), lambda t: (t, 0)),
            pl.BlockSpec((POOL_HALO, d), lambda t: (jnp.maximum(t * hb - 1, 0), 0)),
            pl.BlockSpec((POOL_HALO, d), lambda t: (jnp.minimum((t + 1) * hb, n_hblocks - 1), 0)),
            pl.BlockSpec((None, N_MOD, d), lambda t: (mod_index(t), 0, 0)),
            pl.BlockSpec((1, d), lambda t: (0, 0)),
            pl.BlockSpec(w_pool.shape, lambda t: (0, 0, 0)),
            pl.BlockSpec((1, d), lambda t: (0, 0)),
        ],
        out_specs=pl.BlockSpec((tm, d), lambda t: (t, 0)),
        out_shape=jax.ShapeDtypeStruct(xs.shape, F32),
        scratch_shapes=[pltpu.VMEM((tm + 2 * POOL_HALO, d), F32)],
        compiler_params=_params("parallel"),
        name="pool_mixer",
    )(xs, xs, xs, mod, g, w_pool, scale)


def kernel(x, c, ctx, c_ctx, w_mod, b_mod, g_norm, w_up, w_down, attn_wq, attn_wk, attn_wv, attn_wo, attn_sink, gmlp_w_in, gmlp_g_v, gmlp_w_s, gmlp_b_s, gmlp_w_out, pool_w, pool_scale, g_final):
    B, S, D = x.shape
    L = ctx.shape[1]
    depth = w_mod.shape[0]
    n_q = attn_sink.shape[1]
    hd = attn_wq.shape[2] // n_q
    n_kv = attn_wk.shape[2] // hd
    lat_rows, ctx_rows = B * S, B * L

    tm = min(1024, S, ctx_rows)
    tm_qkv = min(512, S, ctx_rows)
    tm_proj = min(512, S, ctx_rows)
    tm_pool = min(256, S, L)
    assert S % tm == 0 and ctx_rows % tm == 0 and S % ATT_BLOCK == 0 and L % ATT_BLOCK == 0
    assert lat_rows % L == 0 and S // ATT_BLOCK >= 2 and S % GRID_W == 0

    def mod_index_for(tile):
        return lambda t: jnp.minimum(t // (S // tile), B)

    assert N_MIXERS == 3 and depth >= 1
    xs = None

    n_cond = -(-(B + 1) // SUBLANE) * SUBLANE
    cond = jnp.concatenate([c, c_ctx[None, :], jnp.zeros((n_cond - B - 1, D), F32)], axis=0)
    mods = _modulation(cond, w_mod, b_mod).reshape(depth, n_cond, N_MOD, D)

    tables = _rope_tables(S, tm_qkv, hd, hd ** -0.5 * LOG2_E)
    bias = _attn_bias(n_q // n_kv, L)
    qperm, kperm = _head_perm(n_q, hd), _head_perm(n_kv, hd)

    def table_index(t):
        return jnp.where(t < lat_rows // tm_qkv, t % (S // tm_qkv), S // tm_qkv)

    w_up_bf, w_down_bf = w_up[0].astype(BF16), w_down[0].astype(BF16)
    attn_wo_bf = attn_wo.astype(BF16)
    gmlp_w_in_bf, gmlp_w_s_bf, gmlp_w_out_bf = gmlp_w_in.astype(BF16), gmlp_w_s.astype(BF16), gmlp_w_out.astype(BF16)

    out = None
    for i in range(depth):
        kind, j = i % N_MIXERS, i // N_MIXERS
        last = i == depth - 1
        mod = mods[i]
        g_a, g_m = g_norm[i, 0][None, :], g_norm[i, 1][None, :]
        res_rows = lat_rows if last else lat_rows + ctx_rows

        if kind == 0:
            wqkv = jnp.concatenate([attn_wq[j][:, qperm], attn_wk[j][:, kperm], attn_wv[j]], axis=1).astype(BF16)
            streams = (x.reshape(lat_rows, D), ctx.reshape(ctx_rows, D)) if i == 0 else (xs,)
            qkv_out = _qkv(streams, mod, g_a, wqkv, tables, tm=tm_qkv, n_q=n_q, n_kv=n_kv, hd=hd,
                           mod_index=mod_index_for(tm_qkv), table_index=table_index)
            q, k, v = qkv_out[:3]
            if i == 0:
                xs = qkv_out[3]
            o = _attention(q, k, v, attn_sink[j] * LOG2_E, bias, batch=B, seq=S, ctx_len=L,
                           n_q=n_q, n_kv=n_kv, hd=hd, ctx_out=not last)
            xs = _proj_residual(o, attn_wo_bf, xs, mod, layer=j, rows=res_rows, tm=tm_proj,
                                mod_index=mod_index_for(tm_proj))
        elif kind == 1:
            z = _gmlp_gate(xs, mod, g_a, gmlp_w_in_bf, gmlp_g_v, gmlp_w_s_bf, gmlp_b_s,
                           layer=j, tm=tm, mod_index=mod_index_for(tm))
            xs = _proj_residual(z, gmlp_w_out_bf, xs, mod, layer=j, rows=res_rows, tm=tm_proj,
                                mod_index=mod_index_for(tm_proj))
        else:
            xs = _pool_residual(xs, mod, g_a, pool_w[j].astype(BF16), pool_scale[j][None, :],
                                tm=tm_pool, lat_rows=lat_rows, seq=S, ctx_len=L,
                                mod_index=mod_index_for(tm_pool))

        res = _mlp(xs, mod, g_m, w_up_bf, w_down_bf, g_final[None, :], rows=res_rows, tm=tm,
                   mod_index=mod_index_for(tm), final=last,
                   next_weights=None if last else (w_up, w_down, i + 1))
        if last:
            out = res[0]
        else:
            xs, w_up_bf, w_down_bf = res
    return out.reshape(B, S, D)
```

```python
import functools

import jax
import jax.numpy as jnp
from jax import lax
from jax.experimental import pallas as pl
from jax.experimental.pallas import tpu as pltpu

EPS = 1e-6
N_MOD = 6
N_MIXERS = 3
GRID_W = 64
ATT_BLOCK = 128
ROPE_BASE = 10000.0
LOG2_E = 1.4426950408889634
POOL_SIZES = (2, 4, 8, 16)
POOL_HALO = 8
LANE = 128
SUBLANE = 8
VMEM_LIMIT_BYTES = 56 * 1024 * 1024

MOD_COL_TILE = 1024
MLP_FF_TILE = 1024
MLP_OUT_CHUNK = 512
NORM_ROW_CHUNK = 256
MLP_CAST_STEPS = 128
QKV_HEADS_PER_DOT = 4
GMLP_IN_TILE = 2048
GMLP_COL_CHUNK = 256
GMLP_ROW_BLOCK = 512
STAT_ROW_CHUNK = 128

BF16 = jnp.bfloat16
F32 = jnp.float32


def _params(*sem):
    return pltpu.CompilerParams(dimension_semantics=sem, vmem_limit_bytes=VMEM_LIMIT_BYTES)


def _normmod(x, g, shift, scale):
    ms = jnp.mean(x * x, axis=-1, keepdims=True)
    return (x * lax.rsqrt(ms + EPS)) * g * (1.0 + scale) + shift


def _mod_kernel(c_ref, w_ref, b_ref, o_ref):
    c = c_ref[...]
    s = (c * jax.nn.sigmoid(c)).astype(BF16)
    o_ref[...] = jnp.dot(s, w_ref[...].astype(BF16), preferred_element_type=F32) + b_ref[...]


def _modulation(cond, w_mod, b_mod):
    depth, d, n = w_mod.shape
    r = cond.shape[0]
    tn = MOD_COL_TILE
    return pl.pallas_call(
        _mod_kernel,
        grid=(depth, n // tn),
        in_specs=[
            pl.BlockSpec((r, d), lambda i, j: (0, 0)),
            pl.BlockSpec((None, d, tn), lambda i, j: (i, 0, j)),
            pl.BlockSpec((None, 1, tn), lambda i, j: (i, 0, j)),
        ],
        out_specs=pl.BlockSpec((None, r, tn), lambda i, j: (i, 0, j)),
        out_shape=jax.ShapeDtypeStruct((depth, r, n), F32),
        compiler_params=_params("parallel", "parallel"),
        name="modulation",
    )(cond, w_mod, b_mod.reshape(depth, 1, n))


def _mlp_kernel(*refs, final, n_cast):
    if n_cast:
        (x_hbm, mod_ref, g_ref, wu_ref, wd_ref, gf_ref, wun_ref, wdn_ref,
         o_ref, wun_o, wdn_o, x_buf, h_ref, x_sem) = refs
    else:
        x_hbm, mod_ref, g_ref, wu_ref, wd_ref, gf_ref, o_ref, x_buf, h_ref, x_sem = refs
    t, f = pl.program_id(0), pl.program_id(1)
    nt, nf = pl.num_programs(0), pl.num_programs(1)
    tm, d = o_ref.shape

    if n_cast:
        @pl.when(t * nf + f < n_cast)
        def _cast_next_layer_weights():
            wun_o[...] = wun_ref[...].astype(BF16)
            wdn_o[...] = wdn_ref[...].astype(BF16)

    def x_copy(tile):
        return pltpu.make_async_copy(x_hbm.at[pl.ds(tile * tm, tm), :], x_buf, x_sem)

    def up(h):
        u = jnp.dot(h, wu_ref[...], preferred_element_type=F32)
        return jnp.square(jnp.maximum(u, 0.0)).astype(BF16)

    def gate(cols):
        return mod_ref[5:6, cols]

    @pl.when(jnp.logical_and(t == 0, f == 0))
    def _fetch_first_tile():
        x_copy(0).start()

    @pl.when(f == 0)
    def _first():
        x_copy(t).wait()
        for r0 in range(0, tm, NORM_ROW_CHUNK):
            rows = slice(r0, r0 + NORM_ROW_CHUNK)
            h = _normmod(x_buf[rows, :], g_ref[...], mod_ref[3:4, :], mod_ref[4:5, :]).astype(BF16)
            h_ref[rows, :] = h
            u = up(h)
            for n0 in range(0, d, MLP_OUT_CHUNK):
                cols = slice(n0, n0 + MLP_OUT_CHUNK)
                y = jnp.dot(u, wd_ref[:, cols], preferred_element_type=F32)
                o_ref[rows, cols] = x_buf[rows, cols] + gate(cols) * y

    @pl.when(jnp.logical_and(f == 1, t + 1 < nt))
    def _prefetch_next_tile():
        x_copy(t + 1).start()

    @pl.when(jnp.logical_and(f > 0, f < nf - 1))
    def _middle():
        u = up(h_ref[...])
        for n0 in range(0, d, MLP_OUT_CHUNK):
            cols = slice(n0, n0 + MLP_OUT_CHUNK)
            o_ref[:, cols] += gate(cols) * jnp.dot(u, wd_ref[:, cols], preferred_element_type=F32)

    @pl.when(f == nf - 1)
    def _last():
        for r0 in range(0, tm, NORM_ROW_CHUNK):
            rows = slice(r0, r0 + NORM_ROW_CHUNK)
            u = up(h_ref[rows, :])
            y = o_ref[rows, :] + mod_ref[5:6, :] * jnp.dot(u, wd_ref[...], preferred_element_type=F32)
            if final:
                ms = jnp.mean(y * y, axis=-1, keepdims=True)
                y = (y * lax.rsqrt(ms + EPS)) * gf_ref[...]
            o_ref[rows, :] = y


def _cast_side_job(weights, n_steps, step_of):
    wu32, wd32, layer = weights
    _, d, ff = wu32.shape
    n_cast = MLP_CAST_STEPS
    while n_cast > n_steps:
        n_cast //= 2
    cu, cd = d // n_cast, ff // n_cast
    assert d % n_cast == 0 and cu % (2 * SUBLANE) == 0

    def slab(*idx):
        return jnp.minimum(step_of(*idx), n_cast - 1)
    in_specs = [pl.BlockSpec((None, cu, ff), lambda *idx: (layer, slab(*idx), 0)),
                pl.BlockSpec((None, cd, d), lambda *idx: (layer, slab(*idx), 0))]
    out_specs = [pl.BlockSpec((cu, ff), lambda *idx: (slab(*idx), 0)),
                 pl.BlockSpec((cd, d), lambda *idx: (slab(*idx), 0))]
    out_shape = [jax.ShapeDtypeStruct((d, ff), BF16), jax.ShapeDtypeStruct((ff, d), BF16)]
    return n_cast, in_specs, out_specs, out_shape


def _mlp(xs, mod, g, w_up, w_down, g_final, *, rows, tm, mod_index, final, next_weights=None):
    d, ff = w_up.shape
    tf = MLP_FF_TILE
    nt, nf = rows // tm, ff // tf
    assert nf >= 3 and tm % NORM_ROW_CHUNK == 0
    in_specs = [
        pl.BlockSpec(memory_space=pl.ANY),
        pl.BlockSpec((None, N_MOD, d), lambda t, f: (mod_index(t), 0, 0)),
        pl.BlockSpec((1, d), lambda t, f: (0, 0)),
        pl.BlockSpec((d, tf), lambda t, f: (0, f)),
        pl.BlockSpec((tf, d), lambda t, f: (f, 0)),
        pl.BlockSpec((1, d), lambda t, f: (0, 0)),
    ]
    out_specs = [pl.BlockSpec((tm, d), lambda t, f: (t, 0))]
    out_shape = [jax.ShapeDtypeStruct((rows, d), F32)]
    operands = [xs, mod, g, w_up, w_down, g_final]
    n_cast = 0
    if next_weights is not None:
        n_cast, c_in, c_out, c_shape = _cast_side_job(next_weights, nt * nf, lambda t, f: t * nf + f)
        in_specs += c_in
        out_specs += c_out
        out_shape += c_shape
        operands += list(next_weights[:2])
    return pl.pallas_call(
        functools.partial(_mlp_kernel, final=final, n_cast=n_cast),
        grid=(nt, nf),
        in_specs=in_specs,
        out_specs=out_specs,
        out_shape=out_shape,
        scratch_shapes=[pltpu.VMEM((tm, d), F32), pltpu.VMEM((tm, d), BF16), pltpu.SemaphoreType.DMA(())],
        compiler_params=_params("arbitrary", "arbitrary"),
        name="mlp_final" if final else "mlp",
    )(*operands)


def _qkv_kernel(*refs, n_q, n_kv, hd, lat_tiles):
    if lat_tiles is None:
        x_ref, mod_ref, g_ref, wq_ref, wk_ref, wv_ref, *tabs, q_ref, k_ref, v_ref = refs
    else:
        xl_ref, xc_ref, mod_ref, g_ref, wq_ref, wk_ref, wv_ref, *tabs, q_ref, k_ref, v_ref, xs_ref = refs
        is_lat = pl.program_id(0) < lat_tiles
    q_tabs, k_tabs = tabs[:3], tabs[3:]
    group = QKV_HEADS_PER_DOT
    for r0 in range(0, q_ref.shape[0], NORM_ROW_CHUNK):
        rows = slice(r0, r0 + NORM_ROW_CHUNK)
        if lat_tiles is None:
            x = x_ref[rows, :]
        else:
            x = jnp.where(is_lat, xl_ref[rows, :], xc_ref[rows, :])
            xs_ref[rows, :] = x
        h = _normmod(x, g_ref[...], mod_ref[0:1, :], mod_ref[1:2, :]).astype(BF16)

        def rope(y, tables):
            cos_ref, sin_up_ref, sin_down_ref = tables
            return (y * cos_ref[rows, :]
                    + pltpu.roll(y, hd - hd // 4, axis=1) * sin_up_ref[rows, :]
                    + pltpu.roll(y, hd // 4, axis=1) * sin_down_ref[rows, :])

        for h0 in range(0, n_q, group):
            acc = jnp.dot(h, wq_ref[:, h0 * hd:(h0 + group) * hd], preferred_element_type=F32)
            for j in range(group):
                y = acc[:, j * hd:(j + 1) * hd]
                q_ref[rows, (h0 + j) * hd:(h0 + j + 1) * hd] = rope(y, q_tabs).astype(BF16)
        acc = jnp.dot(h, wk_ref[...], preferred_element_type=F32)
        for j in range(n_kv):
            k_ref[rows, j * hd:(j + 1) * hd] = rope(acc[:, j * hd:(j + 1) * hd], k_tabs).astype(BF16)
        v_ref[rows, :] = jnp.dot(h, wv_ref[...], preferred_element_type=F32).astype(BF16)


def _qkv(streams, mod, g, weights, tables, *, layer, tm, n_q, n_kv, hd, mod_index, table_index):
    split = len(streams) == 2
    rows = sum(s.shape[0] for s in streams)
    d = streams[0].shape[1]
    lat_tiles = streams[0].shape[0] // tm if split else None
    tab_spec = pl.BlockSpec((None, tm, hd), lambda t: (table_index(t), 0, 0))
    if split:
        x_specs = [pl.BlockSpec((tm, d), lambda t: (jnp.minimum(t, lat_tiles - 1), 0)),
                   pl.BlockSpec((tm, d), lambda t: (jnp.maximum(t - lat_tiles, 0), 0))]
    else:
        x_specs = [pl.BlockSpec((tm, d), lambda t: (t, 0))]
    out_specs = [
        pl.BlockSpec((tm, n_q * hd), lambda t: (t, 0)),
        pl.BlockSpec((tm, n_kv * hd), lambda t: (t, 0)),
        pl.BlockSpec((tm, n_kv * hd), lambda t: (t, 0)),
    ]
    out_shape = [
        jax.ShapeDtypeStruct((rows, n_q * hd), BF16),
        jax.ShapeDtypeStruct((rows, n_kv * hd), BF16),
        jax.ShapeDtypeStruct((rows, n_kv * hd), BF16),
    ]
    if split:
        out_specs.append(pl.BlockSpec((tm, d), lambda t: (t, 0)))
        out_shape.append(jax.ShapeDtypeStruct((rows, d), F32))
    return pl.pallas_call(
        functools.partial(_qkv_kernel, n_q=n_q, n_kv=n_kv, hd=hd, lat_tiles=lat_tiles),
        grid=(rows // tm,),
        in_specs=x_specs + [
            pl.BlockSpec((None, N_MOD, d), lambda t: (mod_index(t), 0, 0)),
            pl.BlockSpec((1, d), lambda t: (0, 0)),
        ] + [pl.BlockSpec((None,) + w.shape[1:], lambda t: (layer, 0, 0), pipeline_mode=pl.Buffered(1))
             for w in weights] + [
            tab_spec, tab_spec, tab_spec, tab_spec, tab_spec, tab_spec,
        ],
        out_specs=out_specs,
        out_shape=out_shape,
        compiler_params=_params("parallel"),
        name="qkv_rope_join" if split else "qkv_rope",
    )(*streams, mod, g, *weights, *tables)


def _attn_kernel(sink_ref, q_ref, kp_ref, kc_ref, kn_ref, kx_ref, vp_ref, vc_ref, vn_ref, vx_ref,
                 bias_ref, *rest, n_kv, grp, hd, n_cast):
    if n_cast:
        wu32_ref, wd32_ref, o_ref, wu_o, wd_o = rest

        @pl.when(pl.program_id(0) * pl.num_programs(1) + pl.program_id(1) < n_cast)
        def _cast_mlp_weights():
            wu_o[...] = wu32_ref[...].astype(BF16)
            wd_o[...] = wd32_ref[...].astype(BF16)
    else:
        o_ref, = rest
    bias = bias_ref[...]
    blk = q_ref.shape[0]
    n_keys = bias.shape[1]
    ones = jnp.ones((n_keys, hd), BF16)
    for hk in range(n_kv):
        sl = slice(hk * hd, (hk + 1) * hd)
        k_all = jnp.concatenate([kp_ref[:, sl], kc_ref[:, sl], kn_ref[:, sl], kx_ref[:, sl]], axis=0)
        v_all = jnp.concatenate([vp_ref[:, sl], vc_ref[:, sl], vn_ref[:, sl], vx_ref[:, sl]], axis=0)
        v_aug = jnp.concatenate([v_all, ones], axis=1)
        heads = [hk * grp + j for j in range(grp)]
        qh = jnp.concatenate([q_ref[:, hq * hd:(hq + 1) * hd] for hq in heads], axis=0)
        s = lax.dot_general(qh, k_all, (((1,), (1,)), ((), ())), preferred_element_type=F32) + bias
        sink = jnp.concatenate([jnp.full((blk, 1), sink_ref[hq], F32) for hq in heads], axis=0)
        m = jnp.maximum(jnp.max(s, axis=-1, keepdims=True), sink)
        p = jnp.exp2(s - m).astype(BF16)
        ov = jnp.dot(p, v_aug, preferred_element_type=F32)
        o = ov[:, :hd] / (ov[:, hd:] + jnp.exp2(sink - m))
        for j, hq in enumerate(heads):
            o_ref[:, hq * hd:(hq + 1) * hd] = o[j * blk:(j + 1) * blk, :].astype(o_ref.dtype)


def _attention(q, k, v, sink, bias, *, batch, seq, ctx_len, n_q, n_kv, hd, ctx_out, cast_weights=None):
    rows = q.shape[0]
    blk = ATT_BLOCK
    nb = seq // blk
    ncb = ctx_len // blk
    lat_blocks = batch * nb
    nq_blocks = nb + (ncb if ctx_out else 0)

    def q_idx(b, n):
        return (jnp.where(n < nb, b * nb + n, lat_blocks + b * ncb + (n - nb)), 0)

    def kv_idx(off):
        return lambda b, n: (b * nb + jnp.clip(n + off, 0, nb - 1), 0)

    def ctx_idx(b, n):
        return (lat_blocks * blk // ctx_len + b, 0)

    def bias_idx(b, n):
        return (jnp.where(n >= nb, 3, jnp.where(n == 0, 0, jnp.where(n == nb - 1, 2, 1))), 0, 0)

    kv_w = n_kv * hd
    kv_specs = [pl.BlockSpec((blk, kv_w), kv_idx(-1)), pl.BlockSpec((blk, kv_w), kv_idx(0)),
                pl.BlockSpec((blk, kv_w), kv_idx(1)), pl.BlockSpec((ctx_len, kv_w), ctx_idx)]
    in_specs = ([pl.BlockSpec(memory_space=pltpu.SMEM), pl.BlockSpec((blk, n_q * hd), q_idx)]
                + kv_specs + kv_specs + [pl.BlockSpec((None,) + bias.shape[1:], bias_idx)])
    out_specs = [pl.BlockSpec((blk, n_q * hd), q_idx)]
    out_shape = [jax.ShapeDtypeStruct((rows if ctx_out else lat_blocks * blk, n_q * hd), BF16)]
    operands = [sink, q, k, k, k, k, v, v, v, v, bias]
    n_cast = 0
    if cast_weights is not None:
        n_cast, c_in, c_out, c_shape = _cast_side_job(cast_weights, batch * nq_blocks,
                                                      lambda b, n: b * nq_blocks + n)
        in_specs += c_in
        out_specs += c_out
        out_shape += c_shape
        operands += list(cast_weights[:2])
    semantics = ("arbitrary", "arbitrary") if n_cast else ("parallel", "parallel")
    return pl.pallas_call(
        functools.partial(_attn_kernel, n_kv=n_kv, grp=n_q // n_kv, hd=hd, n_cast=n_cast),
        grid=(batch, nq_blocks),
        in_specs=in_specs,
        out_specs=out_specs,
        out_shape=out_shape,
        compiler_params=_params(*semantics),
        name="window_attention",
    )(*operands)


def _proj_res_kernel(a_ref, w_ref, x_ref, mod_ref, o_ref):
    a = a_ref[...]
    for n0 in range(0, o_ref.shape[1], MLP_OUT_CHUNK):
        cols = slice(n0, n0 + MLP_OUT_CHUNK)
        y = jnp.dot(a, w_ref[:, cols], preferred_element_type=F32)
        o_ref[:, cols] = x_ref[:, cols] + mod_ref[2:3, cols] * y


def _proj_residual(a, w, xs, mod, *, layer, rows, tm, mod_index):
    _, kdim, d = w.shape
    return pl.pallas_call(
        _proj_res_kernel,
        grid=(rows // tm,),
        in_specs=[
            pl.BlockSpec((tm, kdim), lambda t: (t, 0)),
            pl.BlockSpec((None, kdim, d), lambda t: (layer, 0, 0), pipeline_mode=pl.Buffered(1)),
            pl.BlockSpec((tm, d), lambda t: (t, 0)),
            pl.BlockSpec((None, N_MOD, d), lambda t: (mod_index(t), 0, 0)),
        ],
        out_specs=pl.BlockSpec((tm, d), lambda t: (t, 0)),
        out_shape=jax.ShapeDtypeStruct(xs.shape, F32),
        input_output_aliases={2: 0},
        compiler_params=_params("parallel"),
        name="proj_residual",
    )(a, w, xs, mod)


def _rope_tables(seq, tm, hd, scale):
    q = hd // 4
    pos = jnp.arange(seq)
    row = (pos // GRID_W).astype(F32)[:, None]
    col = (pos % GRID_W).astype(F32)[:, None]
    axis_dim = hd // 2
    inv_freq = ROPE_BASE ** (-jnp.arange(0, axis_dim, 2, dtype=F32) / axis_dim)
    a_row, a_col = row * inv_freq, col * inv_freq
    zero = jnp.zeros((seq, q), F32)
    cos = jnp.concatenate([jnp.cos(a_row), jnp.cos(a_row), jnp.cos(a_col), jnp.cos(a_col)], axis=1)
    sin_up = jnp.concatenate([-jnp.sin(a_row), zero, -jnp.sin(a_col), zero], axis=1)
    sin_down = jnp.concatenate([zero, jnp.sin(a_row), zero, jnp.sin(a_col)], axis=1)
    assert cos.shape == (seq, hd)

    def tiles(tab, identity_value):
        ident = jnp.full((1, tm, hd), identity_value, F32)
        return jnp.concatenate([tab.reshape(seq // tm, tm, hd), ident], axis=0)

    k_tabs = (tiles(cos, 1.0), tiles(sin_up, 0.0), tiles(sin_down, 0.0))
    return tuple(t * scale for t in k_tabs) + k_tabs


def _attn_bias(grp, ctx_len):
    blk = ATT_BLOCK
    qi = jnp.arange(grp * blk)[:, None] % blk
    kj = jnp.arange(3 * blk + ctx_len)[None, :]
    is_ctx = kj >= 3 * blk
    win = jnp.abs(kj - blk - qi) <= ATT_BLOCK
    variants = [win & (kj >= blk), win, win & (kj < 2 * blk), jnp.zeros_like(win)]
    return jnp.stack([jnp.where(is_ctx | m, 0.0, -jnp.inf).astype(F32) for m in variants])


def _gmlp_gate_kernel(x_hbm, mod_ref, g_ref, w_ref, gv_ref, ws_ref, bs_ref, z_ref,
                      x_buf, h_ref, v_ref, mu_ref, rs_ref, x_sem, *, n_vt, chunk, cg):
    t, j = pl.program_id(0), pl.program_id(1)
    nt = pl.num_programs(0)
    tm, tn = z_ref.shape
    width = n_vt * tn
    row_block = min(GMLP_ROW_BLOCK, tm)

    def x_copy(tile):
        return pltpu.make_async_copy(x_hbm.at[pl.ds(tile * tm, tm), :], x_buf, x_sem)

    def gelu_proj(h, cols):
        return jax.nn.gelu(jnp.dot(h, w_ref[:, cols], preferred_element_type=F32))

    def v_cols(h, rows, tile):
        tot = jnp.zeros((h.shape[0], 1), F32)
        for n0 in range(0, tn, GMLP_COL_CHUNK):
            cols = slice(n0, n0 + GMLP_COL_CHUNK)
            y = gelu_proj(h, cols)
            v_ref[tile, rows, cols] = y.astype(BF16)
            tot += jnp.sum(y, axis=-1, keepdims=True)
        return tot

    def for_row_blocks(n_rows, body):
        def step(i, carry):
            body(pl.multiple_of(i * n_rows, n_rows))
            return carry
        lax.fori_loop(0, tm // n_rows, step, 0)

    def v_stats(base):
        for r0 in range(0, row_block, STAT_ROW_CHUNK):
            rows = pl.ds(base + r0, STAT_ROW_CHUNK)
            mu = mu_ref[rows, :] / width
            mu_wide = jnp.concatenate([mu] * (tn // LANE), axis=1)
            sq = jnp.zeros((STAT_ROW_CHUNK, 1), F32)
            for k in range(n_vt):
                dv = v_ref[k, rows, :].astype(F32) - mu_wide
                sq += jnp.sum(dv * dv, axis=-1, keepdims=True)
            mu_ref[rows, :] = mu
            rs_ref[rows, :] = jnp.broadcast_to(lax.rsqrt(sq / width + EPS), (STAT_ROW_CHUNK, LANE))

    def u_rows(jj, base):
        rep = cg // LANE
        h = h_ref[pl.ds(base, row_block), :]
        for n0 in range(0, tn, GMLP_COL_CHUNK):
            u = gelu_proj(h, slice(n0, n0 + GMLP_COL_CHUNK))
            for g0 in range(0, GMLP_COL_CHUNK, cg):
                gi = jj * (tn // cg) + (n0 + g0) // cg
                cols = slice(n0 + g0, n0 + g0 + cg)
                w_s = ws_ref[gi]
                b_s = jnp.concatenate([bs_ref[gi]] * rep, axis=1)
                gv = gv_ref[jj, :, cols]
                for r0 in range(0, row_block, chunk):
                    rows = pl.ds(base + r0, chunk)
                    mu = jnp.concatenate([mu_ref[rows, :]] * rep, axis=1)
                    rs = jnp.concatenate([rs_ref[rows, :]] * rep, axis=1)
                    vn = ((v_ref[jj, rows, cols].astype(F32) - mu) * rs * gv).astype(BF16)
                    sv = jnp.dot(w_s, vn, preferred_element_type=F32) + b_s
                    z_ref[rows, cols] = (u[r0:r0 + chunk, g0:g0 + cg] * sv).astype(BF16)

    @pl.when(jnp.logical_and(t == 0, j == 0))
    def _fetch_first_tile():
        x_copy(0).start()

    @pl.when(j == 0)
    def _first_v_tile():
        x_copy(t).wait()

        def body(base):
            for r0 in range(0, row_block, NORM_ROW_CHUNK):
                rows = pl.ds(base + r0, NORM_ROW_CHUNK)
                h = _normmod(x_buf[rows, :], g_ref[...], mod_ref[0:1, :], mod_ref[1:2, :]).astype(BF16)
                h_ref[rows, :] = h
                mu_ref[rows, :] = jnp.broadcast_to(v_cols(h, rows, 0), (NORM_ROW_CHUNK, LANE))
        for_row_blocks(row_block, body)

    @pl.when(jnp.logical_and(j == 1, t + 1 < nt))
    def _prefetch_next_tile():
        x_copy(t + 1).start()

    @pl.when(jnp.logical_and(j > 0, j < n_vt))
    def _v_tile():
        def body(base):
            rows = pl.ds(base, row_block)
            mu_ref[rows, :] += jnp.broadcast_to(v_cols(h_ref[rows, :], rows, j), (row_block, LANE))
        for_row_blocks(row_block, body)

    @pl.when(j == n_vt)
    def _first_u_tile():
        def body(base):
            v_stats(base)
            u_rows(0, base)
        for_row_blocks(row_block, body)

    @pl.when(j > n_vt)
    def _u_tile():
        for_row_blocks(row_block, functools.partial(u_rows, j - n_vt))


def _gmlp_gate(xs, mod, g, w_in, g_v, w_s, b_s, *, layer, tm, mod_index):
    rows, d = xs.shape
    tn = GMLP_IN_TILE
    width = w_in.shape[2] // 2
    n_vt = width // tn
    groups, chunk = w_s.shape[1], w_s.shape[2]
    cg = width // groups
    assert tn % GMLP_COL_CHUNK == 0 and GMLP_COL_CHUNK % cg == 0 and cg % LANE == 0 and tm % chunk == 0
    assert n_vt >= 1 and min(GMLP_ROW_BLOCK, tm) % NORM_ROW_CHUNK == 0
    g_v_tiles = g_v[layer].reshape(n_vt, 1, tn)
    b_s_wide = jnp.broadcast_to(b_s[layer][:, :, None], (groups, chunk, LANE))
    return pl.pallas_call(
        functools.partial(_gmlp_gate_kernel, n_vt=n_vt, chunk=chunk, cg=cg),
        grid=(rows // tm, 2 * n_vt),
        in_specs=[
            pl.BlockSpec(memory_space=pl.ANY),
            pl.BlockSpec((None, N_MOD, d), lambda t, j: (mod_index(t), 0, 0)),
            pl.BlockSpec((1, d), lambda t, j: (0, 0)),
            pl.BlockSpec((None, d, tn), lambda t, j: (layer, 0, (j + n_vt) % (2 * n_vt))),
            pl.BlockSpec((n_vt, 1, tn), lambda t, j: (0, 0, 0)),
            pl.BlockSpec((None, groups, chunk, chunk), lambda t, j: (layer, 0, 0, 0)),
            pl.BlockSpec((groups, chunk, LANE), lambda t, j: (0, 0, 0)),
        ],
        out_specs=pl.BlockSpec((tm, tn), lambda t, j: (t, jnp.maximum(j - n_vt, 0))),
        out_shape=jax.ShapeDtypeStruct((rows, width), BF16),
        scratch_shapes=[pltpu.VMEM((tm, d), F32), pltpu.VMEM((tm, d), BF16),
                        pltpu.VMEM((n_vt, tm, tn), BF16),
                        pltpu.VMEM((tm, LANE), F32), pltpu.VMEM((tm, LANE), F32),
                        pltpu.SemaphoreType.DMA(())],
        compiler_params=_params("arbitrary", "arbitrary"),
        name="gmlp_gate",
    )(xs, mod, g, w_in, g_v_tiles, w_s, b_s_wide)


def _pool_kernel(x_ref, xp_ref, xn_ref, mod_ref, g_ref, w_ref, sc_ref, o_ref, e_ref,
                 *, tm, lat_tiles, seq, ctx_len):
    t = pl.program_id(0)
    is_lat = t < lat_tiles
    seq_len = jnp.where(is_lat, seq, ctx_len)
    tiles_per_seq = jnp.where(is_lat, seq // tm, ctx_len // tm)
    p = jnp.where(is_lat, t % (seq // tm), (t - lat_tiles) % (ctx_len // tm))
    has_prev = (p > 0).astype(F32)
    has_next = (p < tiles_per_seq - 1).astype(F32)

    g, sh, sc = g_ref[...], mod_ref[0:1, :], mod_ref[1:2, :]
    x = x_ref[...]
    halo = POOL_HALO
    e_ref[0:halo, :] = _normmod(xp_ref[...], g, sh, sc) * has_prev
    e_ref[halo:halo + tm, :] = _normmod(x, g, sh, sc)
    e_ref[halo + tm:2 * halo + tm, :] = _normmod(xn_ref[...], g, sh, sc) * has_next

    pos = p * tm + lax.broadcasted_iota(jnp.int32, (tm, 1), 0)
    n_ext = tm + 2 * halo
    cgw = x.shape[1] // len(POOL_SIZES)
    for gi, w in enumerate(POOL_SIZES):
        cols = slice(gi * cgw, (gi + 1) * cgw)
        e = e_ref[:, cols]
        a = e + pltpu.roll(e, 1, axis=0)
        step = 1
        while 2 * step < w:
            a = pltpu.roll(a, step, axis=0) + pltpu.roll(a, n_ext - step, axis=0)
            step *= 2
        cnt = (jnp.minimum(pos + w // 2, seq_len) - jnp.maximum(pos - w // 2, 0)).astype(F32)
        dlt = a[halo:halo + tm, :] / cnt - e[halo:halo + tm, :]
        y = jnp.dot(dlt.astype(BF16), w_ref[gi], preferred_element_type=F32) * sc_ref[:, cols]
        o_ref[:, cols] = x[:, cols] + mod_ref[2:3, cols] * y


def _pool_residual(xs, mod, g, w_pool, scale, *, tm, lat_rows, seq, ctx_len, mod_index):
    rows, d = xs.shape
    hb = tm // POOL_HALO
    n_hblocks = rows // POOL_HALO
    return pl.pallas_call(
        functools.partial(_pool_kernel, tm=tm, lat_tiles=lat_rows // tm, seq=seq, ctx_len=ctx_len),
        grid=(rows // tm,),
        in_specs=[
            pl.BlockSpec((tm, d), lambda t: (t, 0)),
            pl.BlockSpec((POOL_HALO, d), lambda t: (jnp.maximum(t * hb - 1, 0), 0)),
            pl.BlockSpec((POOL_HALO, d), lambda t: (jnp.minimum((t + 1) * hb, n_hblocks - 1), 0)),
            pl.BlockSpec((None, N_MOD, d), lambda t: (mod_index(t), 0, 0)),
            pl.BlockSpec((1, d), lambda t: (0, 0)),
            pl.BlockSpec(w_pool.shape, lambda t: (0, 0, 0)),
            pl.BlockSpec((1, d), lambda t: (0, 0)),
        ],
        out_specs=pl.BlockSpec((tm, d), lambda t: (t, 0)),
        out_shape=jax.ShapeDtypeStruct(xs.shape, F32),
        scratch_shapes=[pltpu.VMEM((tm + 2 * POOL_HALO, d), F32)],
        compiler_params=_params("parallel"),
        name="pool_mixer",
    )(xs, xs, xs, mod, g, w_pool, scale)


def kernel(x, c, ctx, c_ctx, w_mod, b_mod, g_norm, w_up, w_down, attn_wq, attn_wk, attn_wv, attn_wo, attn_sink, gmlp_w_in, gmlp_g_v, gmlp_w_s, gmlp_b_s, gmlp_w_out, pool_w, pool_scale, g_final):
    B, S, D = x.shape
    L = ctx.shape[1]
    depth = w_mod.shape[0]
    n_q = attn_sink.shape[1]
    hd = attn_wq.shape[2] // n_q
    n_kv = attn_wk.shape[2] // hd
    lat_rows, ctx_rows = B * S, B * L

    tm = min(1024, S, ctx_rows)
    tm_qkv = min(512, S, ctx_rows)
    tm_proj = min(512, S, ctx_rows)
    tm_pool = min(256, S, L)
    assert S % tm == 0 and ctx_rows % tm == 0 and S % ATT_BLOCK == 0 and L % ATT_BLOCK == 0
    assert lat_rows % L == 0 and S // ATT_BLOCK >= 2 and S % GRID_W == 0

    def mod_index_for(tile):
        return lambda t: jnp.minimum(t // (S // tile), B)

    assert N_MIXERS == 3 and depth >= 1
    xs = None

    n_cond = -(-(B + 1) // SUBLANE) * SUBLANE
    cond = jnp.concatenate([c, c_ctx[None, :], jnp.zeros((n_cond - B - 1, D), F32)], axis=0)
    mods = _modulation(cond, w_mod, b_mod).reshape(depth, n_cond, N_MOD, D)

    tables = _rope_tables(S, tm_qkv, hd, hd ** -0.5 * LOG2_E)
    bias = _attn_bias(n_q // n_kv, L)

    def table_index(t):
        return jnp.where(t < lat_rows // tm_qkv, t % (S // tm_qkv), S // tm_qkv)

    w_up_bf = w_down_bf = None
    attn_wo_bf = attn_wo.astype(BF16)
    attn_qkv_bf = (attn_wq.astype(BF16), attn_wk.astype(BF16), attn_wv.astype(BF16))
    gmlp_w_in_bf, gmlp_w_s_bf, gmlp_w_out_bf = gmlp_w_in.astype(BF16), gmlp_w_s.astype(BF16), gmlp_w_out.astype(BF16)

    out = None
    for i in range(depth):
        kind, j = i % N_MIXERS, i // N_MIXERS
        last = i == depth - 1
        mod = mods[i]
        g_a, g_m = g_norm[i, 0][None, :], g_norm[i, 1][None, :]
        res_rows = lat_rows if last else lat_rows + ctx_rows

        if kind == 0:
            streams = (x.reshape(lat_rows, D), ctx.reshape(ctx_rows, D)) if i == 0 else (xs,)
            qkv_out = _qkv(streams, mod, g_a, attn_qkv_bf, tables, layer=j, tm=tm_qkv, n_q=n_q, n_kv=n_kv,
                           hd=hd, mod_index=mod_index_for(tm_qkv), table_index=table_index)
            q, k, v = qkv_out[:3]
            if i == 0:
                xs = qkv_out[3]
            att = _attention(q, k, v, attn_sink[j] * LOG2_E, bias, batch=B, seq=S, ctx_len=L,
                             n_q=n_q, n_kv=n_kv, hd=hd, ctx_out=not last,
                             cast_weights=(w_up, w_down, 0) if i == 0 else None)
            o = att[0]
            if i == 0:
                w_up_bf, w_down_bf = att[1:]
            xs = _proj_residual(o, attn_wo_bf, xs, mod, layer=j, rows=res_rows, tm=tm_proj,
                                mod_index=mod_index_for(tm_proj))
        elif kind == 1:
            z = _gmlp_gate(xs, mod, g_a, gmlp_w_in_bf, gmlp_g_v, gmlp_w_s_bf, gmlp_b_s,
                           layer=j, tm=tm, mod_index=mod_index_for(tm))
            xs = _proj_residual(z, gmlp_w_out_bf, xs, mod, layer=j, rows=res_rows, tm=tm_proj,
                                mod_index=mod_index_for(tm_proj))
        else:
            xs = _pool_residual(xs, mod, g_a, pool_w[j].astype(BF16), pool_scale[j][None, :],
                                tm=tm_pool, lat_rows=lat_rows, seq=S, ctx_len=L,
                                mod_index=mod_index_for(tm_pool))

        res = _mlp(xs, mod, g_m, w_up_bf, w_down_bf, g_final[None, :], rows=res_rows, tm=tm,
                   mod_index=mod_index_for(tm), final=last,
                   next_weights=None if last else (w_up, w_down, i + 1))
        if last:
            out = res[0]
        else:
            xs, w_up_bf, w_down_bf = res
    return out.reshape(B, S, D)
```

```python
import functools

import jax
import jax.numpy as jnp
from jax import lax
from jax.experimental import pallas as pl
from jax.experimental.pallas import tpu as pltpu

EPS = 1e-6
N_MOD = 6
N_MIXERS = 3
GRID_W = 64
ATT_BLOCK = 128
ROPE_BASE = 10000.0
LOG2_E = 1.4426950408889634
POOL_SIZES = (2, 4, 8, 16)
POOL_HALO = 8
LANE = 128
SUBLANE = 8
VMEM_LIMIT_BYTES = 56 * 1024 * 1024

MOD_COL_TILE = 1024
MLP_FF_TILE = 1024
MLP_OUT_CHUNK = 512
NORM_ROW_CHUNK = 256
MLP_CAST_STEPS = 128
QKV_HEADS_PER_DOT = 4
GMLP_IN_TILE = 2048
GMLP_COL_CHUNK = 256
GMLP_ROW_BLOCK = 512
STAT_ROW_CHUNK = 128

BF16 = jnp.bfloat16
F32 = jnp.float32


def _params(*sem):
    return pltpu.CompilerParams(dimension_semantics=sem, vmem_limit_bytes=VMEM_LIMIT_BYTES)


def _normmod(x, g, shift, scale):
    ms = jnp.mean(x * x, axis=-1, keepdims=True)
    return (x * lax.rsqrt(ms + EPS)) * g * (1.0 + scale) + shift


def _mod_kernel(c_ref, w_ref, b_ref, o_ref):
    c = c_ref[...]
    s = (c * jax.nn.sigmoid(c)).astype(BF16)
    o_ref[...] = jnp.dot(s, w_ref[...].astype(BF16), preferred_element_type=F32) + b_ref[...]


def _modulation(cond, w_mod, b_mod):
    depth, d, n = w_mod.shape
    r = cond.shape[0]
    tn = MOD_COL_TILE
    return pl.pallas_call(
        _mod_kernel,
        grid=(depth, n // tn),
        in_specs=[
            pl.BlockSpec((r, d), lambda i, j: (0, 0)),
            pl.BlockSpec((None, d, tn), lambda i, j: (i, 0, j)),
            pl.BlockSpec((None, 1, tn), lambda i, j: (i, 0, j)),
        ],
        out_specs=pl.BlockSpec((None, r, tn), lambda i, j: (i, 0, j)),
        out_shape=jax.ShapeDtypeStruct((depth, r, n), F32),
        compiler_params=_params("parallel", "parallel"),
        name="modulation",
    )(cond, w_mod, b_mod.reshape(depth, 1, n))


def _mlp_kernel(*refs, final, n_cast):
    if n_cast:
        (x_hbm, mod_ref, g_ref, wu_ref, wd_ref, gf_ref, wun_ref, wdn_ref,
         o_ref, wun_o, wdn_o, x_buf, h_ref, x_sem) = refs
    else:
        x_hbm, mod_ref, g_ref, wu_ref, wd_ref, gf_ref, o_ref, x_buf, h_ref, x_sem = refs
    t, f = pl.program_id(0), pl.program_id(1)
    nt, nf = pl.num_programs(0), pl.num_programs(1)
    tm, d = o_ref.shape

    if n_cast:
        @pl.when(t * nf + f < n_cast)
        def _cast_next_layer_weights():
            wun_o[...] = wun_ref[...].astype(BF16)
            wdn_o[...] = wdn_ref[...].astype(BF16)

    def x_copy(tile):
        return pltpu.make_async_copy(x_hbm.at[pl.ds(tile * tm, tm), :], x_buf, x_sem)

    def up(h):
        u = jnp.dot(h, wu_ref[...], preferred_element_type=F32)
        return jnp.square(jnp.maximum(u, 0.0)).astype(BF16)

    def gate(cols):
        return mod_ref[5:6, cols]

    @pl.when(jnp.logical_and(t == 0, f == 0))
    def _fetch_first_tile():
        x_copy(0).start()

    @pl.when(f == 0)
    def _first():
        x_copy(t).wait()
        for r0 in range(0, tm, NORM_ROW_CHUNK):
            rows = slice(r0, r0 + NORM_ROW_CHUNK)
            h = _normmod(x_buf[rows, :], g_ref[...], mod_ref[3:4, :], mod_ref[4:5, :]).astype(BF16)
            h_ref[rows, :] = h
            u = up(h)
            for n0 in range(0, d, MLP_OUT_CHUNK):
                cols = slice(n0, n0 + MLP_OUT_CHUNK)
                y = jnp.dot(u, wd_ref[:, cols], preferred_element_type=F32)
                o_ref[rows, cols] = x_buf[rows, cols] + gate(cols) * y

    @pl.when(jnp.logical_and(f == 1, t + 1 < nt))
    def _prefetch_next_tile():
        x_copy(t + 1).start()

    @pl.when(jnp.logical_and(f > 0, f < nf - 1))
    def _middle():
        u = up(h_ref[...])
        for n0 in range(0, d, MLP_OUT_CHUNK):
            cols = slice(n0, n0 + MLP_OUT_CHUNK)
            o_ref[:, cols] += gate(cols) * jnp.dot(u, wd_ref[:, cols], preferred_element_type=F32)

    @pl.when(f == nf - 1)
    def _last():
        for r0 in range(0, tm, NORM_ROW_CHUNK):
            rows = slice(r0, r0 + NORM_ROW_CHUNK)
            u = up(h_ref[rows, :])
            y = o_ref[rows, :] + mod_ref[5:6, :] * jnp.dot(u, wd_ref[...], preferred_element_type=F32)
            if final:
                ms = jnp.mean(y * y, axis=-1, keepdims=True)
                y = (y * lax.rsqrt(ms + EPS)) * gf_ref[...]
            o_ref[rows, :] = y


def _cast_side_job(weights, n_steps, step_of):
    wu32, wd32, layer = weights
    _, d, ff = wu32.shape
    n_cast = MLP_CAST_STEPS
    while n_cast > n_steps:
        n_cast //= 2
    cu, cd = d // n_cast, ff // n_cast
    assert d % n_cast == 0 and cu % (2 * SUBLANE) == 0

    def slab(*idx):
        return jnp.minimum(step_of(*idx), n_cast - 1)
    in_specs = [pl.BlockSpec((None, cu, ff), lambda *idx: (layer, slab(*idx), 0)),
                pl.BlockSpec((None, cd, d), lambda *idx: (layer, slab(*idx), 0))]
    out_specs = [pl.BlockSpec((cu, ff), lambda *idx: (slab(*idx), 0)),
                 pl.BlockSpec((cd, d), lambda *idx: (slab(*idx), 0))]
    out_shape = [jax.ShapeDtypeStruct((d, ff), BF16), jax.ShapeDtypeStruct((ff, d), BF16)]
    return n_cast, in_specs, out_specs, out_shape


def _mlp(xs, mod, g, w_up, w_down, g_final, *, rows, tm, mod_index, final, next_weights=None):
    d, ff = w_up.shape
    tf = MLP_FF_TILE
    nt, nf = rows // tm, ff // tf
    assert nf >= 3 and tm % NORM_ROW_CHUNK == 0
    in_specs = [
        pl.BlockSpec(memory_space=pl.ANY),
        pl.BlockSpec((None, N_MOD, d), lambda t, f: (mod_index(t), 0, 0)),
        pl.BlockSpec((1, d), lambda t, f: (0, 0)),
        pl.BlockSpec((d, tf), lambda t, f: (0, f)),
        pl.BlockSpec((tf, d), lambda t, f: (f, 0)),
        pl.BlockSpec((1, d), lambda t, f: (0, 0)),
    ]
    out_specs = [pl.BlockSpec((tm, d), lambda t, f: (t, 0))]
    out_shape = [jax.ShapeDtypeStruct((rows, d), F32)]
    operands = [xs, mod, g, w_up, w_down, g_final]
    n_cast = 0
    if next_weights is not None:
        n_cast, c_in, c_out, c_shape = _cast_side_job(next_weights, nt * nf, lambda t, f: t * nf + f)
        in_specs += c_in
        out_specs += c_out
        out_shape += c_shape
        operands += list(next_weights[:2])
    return pl.pallas_call(
        functools.partial(_mlp_kernel, final=final, n_cast=n_cast),
        grid=(nt, nf),
        in_specs=in_specs,
        out_specs=out_specs,
        out_shape=out_shape,
        scratch_shapes=[pltpu.VMEM((tm, d), F32), pltpu.VMEM((tm, d), BF16), pltpu.SemaphoreType.DMA(())],
        compiler_params=_params("arbitrary", "arbitrary"),
        name="mlp_final" if final else "mlp",
    )(*operands)


def _qkv_kernel(*refs, n_q, n_kv, hd, lat_tiles):
    if lat_tiles is None:
        x_ref, mod_ref, g_ref, wq_ref, wk_ref, wv_ref, *tabs, q_ref, k_ref, v_ref = refs
    else:
        xl_ref, xc_ref, mod_ref, g_ref, wq_ref, wk_ref, wv_ref, *tabs, q_ref, k_ref, v_ref, xs_ref = refs
        is_lat = pl.program_id(0) < lat_tiles
    q_tabs, k_tabs = tabs[:3], tabs[3:]
    group = QKV_HEADS_PER_DOT
    for r0 in range(0, q_ref.shape[0], NORM_ROW_CHUNK):
        rows = slice(r0, r0 + NORM_ROW_CHUNK)
        if lat_tiles is None:
            x = x_ref[rows, :]
        else:
            x = jnp.where(is_lat, xl_ref[rows, :], xc_ref[rows, :])
            xs_ref[rows, :] = x
        h = _normmod(x, g_ref[...], mod_ref[0:1, :], mod_ref[1:2, :]).astype(BF16)

        def rope(y, tables):
            cos_ref, sin_up_ref, sin_down_ref = tables
            return (y * cos_ref[rows, :]
                    + pltpu.roll(y, hd - hd // 4, axis=1) * sin_up_ref[rows, :]
                    + pltpu.roll(y, hd // 4, axis=1) * sin_down_ref[rows, :])

        for h0 in range(0, n_q, group):
            acc = jnp.dot(h, wq_ref[:, h0 * hd:(h0 + group) * hd], preferred_element_type=F32)
            for j in range(group):
                y = acc[:, j * hd:(j + 1) * hd]
                q_ref[rows, (h0 + j) * hd:(h0 + j + 1) * hd] = rope(y, q_tabs).astype(BF16)
        acc = jnp.dot(h, wk_ref[...], preferred_element_type=F32)
        for j in range(n_kv):
            k_ref[rows, j * hd:(j + 1) * hd] = rope(acc[:, j * hd:(j + 1) * hd], k_tabs).astype(BF16)
        v_ref[rows, :] = jnp.dot(h, wv_ref[...], preferred_element_type=F32).astype(BF16)


def _qkv(streams, mod, g, weights, tables, *, layer, tm, n_q, n_kv, hd, mod_index, table_index):
    split = len(streams) == 2
    rows = sum(s.shape[0] for s in streams)
    d = streams[0].shape[1]
    lat_tiles = streams[0].shape[0] // tm if split else None
    tab_spec = pl.BlockSpec((None, tm, hd), lambda t: (table_index(t), 0, 0))
    if split:
        x_specs = [pl.BlockSpec((tm, d), lambda t: (jnp.minimum(t, lat_tiles - 1), 0)),
                   pl.BlockSpec((tm, d), lambda t: (jnp.maximum(t - lat_tiles, 0), 0))]
    else:
        x_specs = [pl.BlockSpec((tm, d), lambda t: (t, 0))]
    out_specs = [
        pl.BlockSpec((tm, n_q * hd), lambda t: (t, 0)),
        pl.BlockSpec((tm, n_kv * hd), lambda t: (t, 0)),
        pl.BlockSpec((tm, n_kv * hd), lambda t: (t, 0)),
    ]
    out_shape = [
        jax.ShapeDtypeStruct((rows, n_q * hd), BF16),
        jax.ShapeDtypeStruct((rows, n_kv * hd), BF16),
        jax.ShapeDtypeStruct((rows, n_kv * hd), BF16),
    ]
    if split:
        out_specs.append(pl.BlockSpec((tm, d), lambda t: (t, 0)))
        out_shape.append(jax.ShapeDtypeStruct((rows, d), F32))
    return pl.pallas_call(
        functools.partial(_qkv_kernel, n_q=n_q, n_kv=n_kv, hd=hd, lat_tiles=lat_tiles),
        grid=(rows // tm,),
        in_specs=x_specs + [
            pl.BlockSpec((None, N_MOD, d), lambda t: (mod_index(t), 0, 0)),
            pl.BlockSpec((1, d), lambda t: (0, 0)),
        ] + [pl.BlockSpec((None,) + w.shape[1:], lambda t: (layer, 0, 0), pipeline_mode=pl.Buffered(1))
             for w in weights] + [
            tab_spec, tab_spec, tab_spec, tab_spec, tab_spec, tab_spec,
        ],
        out_specs=out_specs,
        out_shape=out_shape,
        compiler_params=_params("parallel"),
        name="qkv_rope_join" if split else "qkv_rope",
    )(*streams, mod, g, *weights, *tables)


def _attn_kernel(sink_ref, q_ref, kp_ref, kc_ref, kn_ref, kx_ref, vp_ref, vc_ref, vn_ref, vx_ref,
                 bias_ref, *rest, n_kv, grp, hd, n_cast):
    if n_cast:
        wu32_ref, wd32_ref, o_ref, wu_o, wd_o = rest

        @pl.when(pl.program_id(0) * pl.num_programs(1) + pl.program_id(1) < n_cast)
        def _cast_mlp_weights():
            wu_o[...] = wu32_ref[...].astype(BF16)
            wd_o[...] = wd32_ref[...].astype(BF16)
    else:
        o_ref, = rest
    bias_t = bias_ref[...]
    blk = q_ref.shape[0]
    n_keys = bias_t.shape[0]
    ones = jnp.ones((n_keys, hd), BF16)
    for hk in range(n_kv):
        sl = slice(hk * hd, (hk + 1) * hd)
        k_all = jnp.concatenate([kp_ref[:, sl], kc_ref[:, sl], kn_ref[:, sl], kx_ref[:, sl]], axis=0)
        v_all = jnp.concatenate([vp_ref[:, sl], vc_ref[:, sl], vn_ref[:, sl], vx_ref[:, sl]], axis=0)
        v_aug = jnp.concatenate([v_all, ones], axis=1)
        heads = [hk * grp + j for j in range(grp)]
        qh = jnp.concatenate([q_ref[:, hq * hd:(hq + 1) * hd] for hq in heads], axis=0)
        s_t = lax.dot_general(k_all, qh, (((1,), (1,)), ((), ())), preferred_element_type=F32) + bias_t
        sink = jnp.concatenate([jnp.full((1, blk), sink_ref[hq], F32) for hq in heads], axis=1)
        m = jnp.maximum(jnp.max(s_t, axis=0, keepdims=True), sink)
        p_t = jnp.exp2(s_t - m).astype(BF16)
        ov_t = lax.dot_general(v_aug, p_t, (((0,), (0,)), ((), ())), preferred_element_type=F32)
        o_t = ov_t[:hd, :] / (ov_t[hd:, :] + jnp.exp2(sink - m))
        o = o_t.T
        for j, hq in enumerate(heads):
            o_ref[:, hq * hd:(hq + 1) * hd] = o[j * blk:(j + 1) * blk, :].astype(o_ref.dtype)


def _attention(q, k, v, sink, bias, *, batch, seq, ctx_len, n_q, n_kv, hd, ctx_out, cast_weights=None):
    rows = q.shape[0]
    blk = ATT_BLOCK
    nb = seq // blk
    ncb = ctx_len // blk
    lat_blocks = batch * nb
    nq_blocks = nb + (ncb if ctx_out else 0)

    def q_idx(b, n):
        return (jnp.where(n < nb, b * nb + n, lat_blocks + b * ncb + (n - nb)), 0)

    def kv_idx(off):
        return lambda b, n: (b * nb + jnp.clip(n + off, 0, nb - 1), 0)

    def ctx_idx(b, n):
        return (lat_blocks * blk // ctx_len + b, 0)

    def bias_idx(b, n):
        return (jnp.where(n >= nb, 3, jnp.where(n == 0, 0, jnp.where(n == nb - 1, 2, 1))), 0, 0)

    kv_w = n_kv * hd
    kv_specs = [pl.BlockSpec((blk, kv_w), kv_idx(-1)), pl.BlockSpec((blk, kv_w), kv_idx(0)),
                pl.BlockSpec((blk, kv_w), kv_idx(1)), pl.BlockSpec((ctx_len, kv_w), ctx_idx)]
    in_specs = ([pl.BlockSpec(memory_space=pltpu.SMEM), pl.BlockSpec((blk, n_q * hd), q_idx)]
                + kv_specs + kv_specs + [pl.BlockSpec((None,) + bias.shape[1:], bias_idx)])
    out_specs = [pl.BlockSpec((blk, n_q * hd), q_idx)]
    out_shape = [jax.ShapeDtypeStruct((rows if ctx_out else lat_blocks * blk, n_q * hd), BF16)]
    operands = [sink, q, k, k, k, k, v, v, v, v, bias]
    n_cast = 0
    if cast_weights is not None:
        n_cast, c_in, c_out, c_shape = _cast_side_job(cast_weights, batch * nq_blocks,
                                                      lambda b, n: b * nq_blocks + n)
        in_specs += c_in
        out_specs += c_out
        out_shape += c_shape
        operands += list(cast_weights[:2])
    semantics = ("arbitrary", "arbitrary") if n_cast else ("parallel", "parallel")
    return pl.pallas_call(
        functools.partial(_attn_kernel, n_kv=n_kv, grp=n_q // n_kv, hd=hd, n_cast=n_cast),
        grid=(batch, nq_blocks),
        in_specs=in_specs,
        out_specs=out_specs,
        out_shape=out_shape,
        compiler_params=_params(*semantics),
        name="window_attention",
    )(*operands)


def _proj_res_kernel(a_ref, w_ref, x_ref, mod_ref, o_ref):
    a = a_ref[...]
    for n0 in range(0, o_ref.shape[1], MLP_OUT_CHUNK):
        cols = slice(n0, n0 + MLP_OUT_CHUNK)
        y = jnp.dot(a, w_ref[:, cols], preferred_element_type=F32)
        o_ref[:, cols] = x_ref[:, cols] + mod_ref[2:3, cols] * y


def _proj_residual(a, w, xs, mod, *, layer, rows, tm, mod_index):
    _, kdim, d = w.shape
    return pl.pallas_call(
        _proj_res_kernel,
        grid=(rows // tm,),
        in_specs=[
            pl.BlockSpec((tm, kdim), lambda t: (t, 0)),
            pl.BlockSpec((None, kdim, d), lambda t: (layer, 0, 0), pipeline_mode=pl.Buffered(1)),
            pl.BlockSpec((tm, d), lambda t: (t, 0)),
            pl.BlockSpec((None, N_MOD, d), lambda t: (mod_index(t), 0, 0)),
        ],
        out_specs=pl.BlockSpec((tm, d), lambda t: (t, 0)),
        out_shape=jax.ShapeDtypeStruct(xs.shape, F32),
        input_output_aliases={2: 0},
        compiler_params=_params("parallel"),
        name="proj_residual",
    )(a, w, xs, mod)


def _rope_tables(seq, tm, hd, scale):
    q = hd // 4
    pos = jnp.arange(seq)
    row = (pos // GRID_W).astype(F32)[:, None]
    col = (pos % GRID_W).astype(F32)[:, None]
    axis_dim = hd // 2
    inv_freq = ROPE_BASE ** (-jnp.arange(0, axis_dim, 2, dtype=F32) / axis_dim)
    a_row, a_col = row * inv_freq, col * inv_freq
    zero = jnp.zeros((seq, q), F32)
    cos = jnp.concatenate([jnp.cos(a_row), jnp.cos(a_row), jnp.cos(a_col), jnp.cos(a_col)], axis=1)
    sin_up = jnp.concatenate([-jnp.sin(a_row), zero, -jnp.sin(a_col), zero], axis=1)
    sin_down = jnp.concatenate([zero, jnp.sin(a_row), zero, jnp.sin(a_col)], axis=1)
    assert cos.shape == (seq, hd)

    def tiles(tab, identity_value):
        ident = jnp.full((1, tm, hd), identity_value, F32)
        return jnp.concatenate([tab.reshape(seq // tm, tm, hd), ident], axis=0)

    k_tabs = (tiles(cos, 1.0), tiles(sin_up, 0.0), tiles(sin_down, 0.0))
    return tuple(t * scale for t in k_tabs) + k_tabs


def _attn_bias(grp, ctx_len):
    blk = ATT_BLOCK
    qi = jnp.arange(grp * blk)[:, None] % blk
    kj = jnp.arange(3 * blk + ctx_len)[None, :]
    is_ctx = kj >= 3 * blk
    win = jnp.abs(kj - blk - qi) <= ATT_BLOCK
    variants = [win & (kj >= blk), win, win & (kj < 2 * blk), jnp.zeros_like(win)]
    return jnp.stack([jnp.where(is_ctx | m, 0.0, -jnp.inf).astype(F32).T for m in variants])


def _gmlp_gate_kernel(x_hbm, mod_ref, g_ref, w_ref, gv_ref, ws_ref, bs_ref, z_ref,
                      x_buf, h_ref, v_ref, mu_ref, rs_ref, x_sem, *, n_vt, chunk, cg):
    t, j = pl.program_id(0), pl.program_id(1)
    nt = pl.num_programs(0)
    tm, tn = z_ref.shape
    width = n_vt * tn
    row_block = min(GMLP_ROW_BLOCK, tm)

    def x_copy(tile):
        return pltpu.make_async_copy(x_hbm.at[pl.ds(tile * tm, tm), :], x_buf, x_sem)

    def gelu_proj(h, cols):
        return jax.nn.gelu(jnp.dot(h, w_ref[:, cols], preferred_element_type=F32))

    def v_cols(h, rows, tile):
        tot = jnp.zeros((h.shape[0], 1), F32)
        for n0 in range(0, tn, GMLP_COL_CHUNK):
            cols = slice(n0, n0 + GMLP_COL_CHUNK)
            y = gelu_proj(h, cols)
            v_ref[tile, rows, cols] = y.astype(BF16)
            tot += jnp.sum(y, axis=-1, keepdims=True)
        return tot

    def for_row_blocks(n_rows, body):
        def step(i, carry):
            body(pl.multiple_of(i * n_rows, n_rows))
            return carry
        lax.fori_loop(0, tm // n_rows, step, 0)

    def v_stats(base):
        for r0 in range(0, row_block, STAT_ROW_CHUNK):
            rows = pl.ds(base + r0, STAT_ROW_CHUNK)
            mu = mu_ref[rows, :] / width
            mu_wide = jnp.concatenate([mu] * (tn // LANE), axis=1)
            sq = jnp.zeros((STAT_ROW_CHUNK, 1), F32)
            for k in range(n_vt):
                dv = v_ref[k, rows, :].astype(F32) - mu_wide
                sq += jnp.sum(dv * dv, axis=-1, keepdims=True)
            mu_ref[rows, :] = mu
            rs_ref[rows, :] = jnp.broadcast_to(lax.rsqrt(sq / width + EPS), (STAT_ROW_CHUNK, LANE))

    def u_rows(jj, base):
        rep = cg // LANE
        h = h_ref[pl.ds(base, row_block), :]
        for n0 in range(0, tn, GMLP_COL_CHUNK):
            u = gelu_proj(h, slice(n0, n0 + GMLP_COL_CHUNK))
            for g0 in range(0, GMLP_COL_CHUNK, cg):
                gi = jj * (tn // cg) + (n0 + g0) // cg
                cols = slice(n0 + g0, n0 + g0 + cg)
                w_s = ws_ref[gi]
                b_s = jnp.concatenate([bs_ref[gi]] * rep, axis=1)
                gv = gv_ref[jj, :, cols]
                for r0 in range(0, row_block, chunk):
                    rows = pl.ds(base + r0, chunk)
                    mu = jnp.concatenate([mu_ref[rows, :]] * rep, axis=1)
                    rs = jnp.concatenate([rs_ref[rows, :]] * rep, axis=1)
                    vn = ((v_ref[jj, rows, cols].astype(F32) - mu) * rs * gv).astype(BF16)
                    sv = jnp.dot(w_s, vn, preferred_element_type=F32) + b_s
                    z_ref[rows, cols] = (u[r0:r0 + chunk, g0:g0 + cg] * sv).astype(BF16)

    @pl.when(jnp.logical_and(t == 0, j == 0))
    def _fetch_first_tile():
        x_copy(0).start()

    @pl.when(j == 0)
    def _first_v_tile():
        x_copy(t).wait()

        def body(base):
            for r0 in range(0, row_block, NORM_ROW_CHUNK):
                rows = pl.ds(base + r0, NORM_ROW_CHUNK)
                h = _normmod(x_buf[rows, :], g_ref[...], mod_ref[0:1, :], mod_ref[1:2, :]).astype(BF16)
                h_ref[rows, :] = h
                mu_ref[rows, :] = jnp.broadcast_to(v_cols(h, rows, 0), (NORM_ROW_CHUNK, LANE))
        for_row_blocks(row_block, body)

    @pl.when(jnp.logical_and(j == 1, t + 1 < nt))
    def _prefetch_next_tile():
        x_copy(t + 1).start()

    @pl.when(jnp.logical_and(j > 0, j < n_vt))
    def _v_tile():
        def body(base):
            rows = pl.ds(base, row_block)
            mu_ref[rows, :] += jnp.broadcast_to(v_cols(h_ref[rows, :], rows, j), (row_block, LANE))
        for_row_blocks(row_block, body)

    @pl.when(j == n_vt)
    def _first_u_tile():
        def body(base):
            v_stats(base)
            u_rows(0, base)
        for_row_blocks(row_block, body)

    @pl.when(j > n_vt)
    def _u_tile():
        for_row_blocks(row_block, functools.partial(u_rows, j - n_vt))


def _gmlp_gate(xs, mod, g, w_in, g_v, w_s, b_s, *, layer, tm, mod_index):
    rows, d = xs.shape
    tn = GMLP_IN_TILE
    width = w_in.shape[2] // 2
    n_vt = width // tn
    groups, chunk = w_s.shape[1], w_s.shape[2]
    cg = width // groups
    assert tn % GMLP_COL_CHUNK == 0 and GMLP_COL_CHUNK % cg == 0 and cg % LANE == 0 and tm % chunk == 0
    assert n_vt >= 1 and min(GMLP_ROW_BLOCK, tm) % NORM_ROW_CHUNK == 0
    g_v_tiles = g_v[layer].reshape(n_vt, 1, tn)
    b_s_wide = jnp.broadcast_to(b_s[layer][:, :, None], (groups, chunk, LANE))
    return pl.pallas_call(
        functools.partial(_gmlp_gate_kernel, n_vt=n_vt, chunk=chunk, cg=cg),
        grid=(rows // tm, 2 * n_vt),
        in_specs=[
            pl.BlockSpec(memory_space=pl.ANY),
            pl.BlockSpec((None, N_MOD, d), lambda t, j: (mod_index(t), 0, 0)),
            pl.BlockSpec((1, d), lambda t, j: (0, 0)),
            pl.BlockSpec((None, d, tn), lambda t, j: (layer, 0, (j + n_vt) % (2 * n_vt))),
            pl.BlockSpec((n_vt, 1, tn), lambda t, j: (0, 0, 0)),
            pl.BlockSpec((None, groups, chunk, chunk), lambda t, j: (layer, 0, 0, 0)),
            pl.BlockSpec((groups, chunk, LANE), lambda t, j: (0, 0, 0)),
        ],
        out_specs=pl.BlockSpec((tm, tn), lambda t, j: (t, jnp.maximum(j - n_vt, 0))),
        out_shape=jax.ShapeDtypeStruct((rows, width), BF16),
        scratch_shapes=[pltpu.VMEM((tm, d), F32), pltpu.VMEM((tm, d), BF16),
                        pltpu.VMEM((n_vt, tm, tn), BF16),
                        pltpu.VMEM((tm, LANE), F32), pltpu.VMEM((tm, LANE), F32),
                        pltpu.SemaphoreType.DMA(())],
        compiler_params=_params("arbitrary", "arbitrary"),
        name="gmlp_gate",
    )(xs, mod, g, w_in, g_v_tiles, w_s, b_s_wide)


def _pool_kernel(x_ref, xp_ref, xn_ref, mod_ref, g_ref, w_ref, sc_ref, o_ref, e_ref,
                 *, tm, lat_tiles, seq, ctx_len):
    t = pl.program_id(0)
    is_lat = t < lat_tiles
    seq_len = jnp.where(is_lat, seq, ctx_len)
    tiles_per_seq = jnp.where(is_lat, seq // tm, ctx_len // tm)
    p = jnp.where(is_lat, t % (seq // tm), (t - lat_tiles) % (ctx_len // tm))
    has_prev = (p > 0).astype(F32)
    has_next = (p < tiles_per_seq - 1).astype(F32)

    g, sh, sc = g_ref[...], mod_ref[0:1, :], mod_ref[1:2, :]
    x = x_ref[...]
    halo = POOL_HALO
    e_ref[0:halo, :] = _normmod(xp_ref[...], g, sh, sc) * has_prev
    e_ref[halo:halo + tm, :] = _normmod(x, g, sh, sc)
    e_ref[halo + tm:2 * halo + tm, :] = _normmod(xn_ref[...], g, sh, sc) * has_next

    pos = p * tm + lax.broadcasted_iota(jnp.int32, (tm, 1), 0)
    n_ext = tm + 2 * halo
    cgw = x.shape[1] // len(POOL_SIZES)
    for gi, w in enumerate(POOL_SIZES):
        cols = slice(gi * cgw, (gi + 1) * cgw)
        e = e_ref[:, cols]
        a = e + pltpu.roll(e, 1, axis=0)
        step = 1
        while 2 * step < w:
            a = pltpu.roll(a, step, axis=0) + pltpu.roll(a, n_ext - step, axis=0)
            step *= 2
        cnt = (jnp.minimum(pos + w // 2, seq_len) - jnp.maximum(pos - w // 2, 0)).astype(F32)
        dlt = a[halo:halo + tm, :] / cnt - e[halo:halo + tm, :]
        y = jnp.dot(dlt.astype(BF16), w_ref[gi], preferred_element_type=F32) * sc_ref[:, cols]
        o_ref[:, cols] = x[:, cols] + mod_ref[2:3, cols] * y


def _pool_residual(xs, mod, g, w_pool, scale, *, tm, lat_rows, seq, ctx_len, mod_index):
    rows, d = xs.shape
    hb = tm // POOL_HALO
    n_hblocks = rows // POOL_HALO
    return pl.pallas_call(
        functools.partial(_pool_kernel, tm=tm, lat_tiles=lat_rows // tm, seq=seq, ctx_len=ctx_len),
        grid=(rows // tm,),
        in_specs=[
            pl.BlockSpec((tm, d), lambda t: (t, 0)),
            pl.BlockSpec((POOL_HALO, d), lambda t: (jnp.maximum(t * hb - 1, 0), 0)),
            pl.BlockSpec((POOL_HALO, d), lambda t: (jnp.minimum((t + 1) * hb, n_hblocks - 1), 0)),
            pl.BlockSpec((None, N_MOD, d), lambda t: (mod_index(t), 0, 0)),
            pl.BlockSpec((1, d), lambda t: (0, 0)),
            pl.BlockSpec(w_pool.shape, lambda t: (0, 0, 0)),
            pl.BlockSpec((1, d), lambda t: (0, 0)),
        ],
        out_specs=pl.BlockSpec((tm, d), lambda t: (t, 0)),
        out_shape=jax.ShapeDtypeStruct(xs.shape, F32),
        scratch_shapes=[pltpu.VMEM((tm + 2 * POOL_HALO, d), F32)],
        compiler_params=_params("parallel"),
        name="pool_mixer",
    )(xs, xs, xs, mod, g, w_pool, scale)


def kernel(x, c, ctx, c_ctx, w_mod, b_mod, g_norm, w_up, w_down, attn_wq, attn_wk, attn_wv, attn_wo, attn_sink, gmlp_w_in, gmlp_g_v, gmlp_w_s, gmlp_b_s, gmlp_w_out, pool_w, pool_scale, g_final):
    B, S, D = x.shape
    L = ctx.shape[1]
    depth = w_mod.shape[0]
    n_q = attn_sink.shape[1]
    hd = attn_wq.shape[2] // n_q
    n_kv = attn_wk.shape[2] // hd
    lat_rows, ctx_rows = B * S, B * L

    tm = min(1024, S, ctx_rows)
    tm_qkv = min(512, S, ctx_rows)
    tm_proj = min(512, S, ctx_rows)
    tm_pool = min(256, S, L)
    assert S % tm == 0 and ctx_rows % tm == 0 and S % ATT_BLOCK == 0 and L % ATT_BLOCK == 0
    assert lat_rows % L == 0 and S // ATT_BLOCK >= 2 and S % GRID_W == 0

    def mod_index_for(tile):
        return lambda t: jnp.minimum(t // (S // tile), B)

    assert N_MIXERS == 3 and depth >= 1
    xs = None

    n_cond = -(-(B + 1) // SUBLANE) * SUBLANE
    cond = jnp.concatenate([c, c_ctx[None, :], jnp.zeros((n_cond - B - 1, D), F32)], axis=0)
    mods = _modulation(cond, w_mod, b_mod).reshape(depth, n_cond, N_MOD, D)

    tables = _rope_tables(S, tm_qkv, hd, hd ** -0.5 * LOG2_E)
    bias = _attn_bias(n_q // n_kv, L)

    def table_index(t):
        return jnp.where(t < lat_rows // tm_qkv, t % (S // tm_qkv), S // tm_qkv)

    w_up_bf = w_down_bf = None
    attn_wo_bf = attn_wo.astype(BF16)
    attn_qkv_bf = (attn_wq.astype(BF16), attn_wk.astype(BF16), attn_wv.astype(BF16))
    gmlp_w_in_bf, gmlp_w_s_bf, gmlp_w_out_bf = gmlp_w_in.astype(BF16), gmlp_w_s.astype(BF16), gmlp_w_out.astype(BF16)

    out = None
    for i in range(depth):
        kind, j = i % N_MIXERS, i // N_MIXERS
        last = i == depth - 1
        mod = mods[i]
        g_a, g_m = g_norm[i, 0][None, :], g_norm[i, 1][None, :]
        res_rows = lat_rows if last else lat_rows + ctx_rows

        if kind == 0:
            streams = (x.reshape(lat_rows, D), ctx.reshape(ctx_rows, D)) if i == 0 else (xs,)
            qkv_out = _qkv(streams, mod, g_a, attn_qkv_bf, tables, layer=j, tm=tm_qkv, n_q=n_q, n_kv=n_kv,
                           hd=hd, mod_index=mod_index_for(tm_qkv), table_index=table_index)
            q, k, v = qkv_out[:3]
            if i == 0:
                xs = qkv_out[3]
            att = _attention(q, k, v, attn_sink[j] * LOG2_E, bias, batch=B, seq=S, ctx_len=L,
                             n_q=n_q, n_kv=n_kv, hd=hd, ctx_out=not last,
                             cast_weights=(w_up, w_down, 0) if i == 0 else None)
            o = att[0]
            if i == 0:
                w_up_bf, w_down_bf = att[1:]
            xs = _proj_residual(o, attn_wo_bf, xs, mod, layer=j, rows=res_rows, tm=tm_proj,
                                mod_index=mod_index_for(tm_proj))
        elif kind == 1:
            z = _gmlp_gate(xs, mod, g_a, gmlp_w_in_bf, gmlp_g_v, gmlp_w_s_bf, gmlp_b_s,
                           layer=j, tm=tm, mod_index=mod_index_for(tm))
            xs = _proj_residual(z, gmlp_w_out_bf, xs, mod, layer=j, rows=res_rows, tm=tm_proj,
                                mod_index=mod_index_for(tm_proj))
        else:
            xs = _pool_residual(xs, mod, g_a, pool_w[j].astype(BF16), pool_scale[j][None, :],
                                tm=tm_pool, lat_rows=lat_rows, seq=S, ctx_len=L,
                                mod_index=mod_index_for(tm_pool))

        res = _mlp(xs, mod, g_m, w_up_bf, w_down_bf, g_final[None, :], rows=res_rows, tm=tm,
                   mod_index=mod_index_for(tm), final=last,
                   next_weights=None if last else (w_up, w_down, i + 1))
        if last:
            out = res[0]
        else:
            xs, w_up_bf, w_down_bf = res
    return out.reshape(B, S, D)
```
